```python
import math
import jax, jax.numpy as jnp
from jax import lax
import numpy as np

D_MODEL = 4096
BATCH = 4
SEQ = 4096
DEPTH = 1

GRID_W = 64
CTX_LEN = 256
DN = D_MODEL // 2
NH = 16
HD = DN // NH
CONF = D_MODEL - DN
SHORT_CONV = 7
CONF_K = 31
CHUNK = 64
D_FF = 4 * D_MODEL
MOD_INIT = 0.5
Z_OFF = 3 * DN
G_OFF = 4 * DN
CONF_OFF = 4 * DN + 4 * NH
IN_COLS = CONF_OFF + 2 * CONF

kernel_name = "hybrid_deltanet_conformer_dit_block"


def layer_norm(x, g, b, eps=1e-5):
    xf = x.astype(jnp.float32)
    mu = jnp.mean(xf, axis=-1, keepdims=True)
    var = jnp.mean(jnp.square(xf - mu), axis=-1, keepdims=True)
    return ((xf - mu) * lax.rsqrt(var + eps) * g.astype(jnp.float32) + b.astype(jnp.float32)).astype(x.dtype)


def l2norm(x, eps=1e-6):
    xf = x.astype(jnp.float32)
    return xf * lax.rsqrt(jnp.sum(jnp.square(xf), axis=-1, keepdims=True) + eps)


def modulate(x, shift, scale):
    return x * (1.0 + scale) + shift


def dwconv1d(x, w):
    k = w.shape[0]
    return lax.conv_general_dilated(
        x, w[:, None, :].astype(x.dtype), window_strides=(1,), padding=[(k // 2, k // 2)],
        dimension_numbers=('NWC', 'WIO', 'NWC'), feature_group_count=x.shape[-1])


def gated_delta_chunked(q, k, v, g, beta, s0):
    b_, l_, h_, dk = q.shape
    dv = v.shape[-1]
    n = l_ // CHUNK
    f32 = jnp.float32

    def chunks(t):
        t = t.astype(f32).reshape((b_, n, CHUNK, h_) + t.shape[3:])
        return jnp.moveaxis(t, (1, 3), (0, 2))

    q = chunks(q) * (dk ** -0.5)
    k, v, beta = chunks(k), chunks(v), chunks(beta)
    gc = jnp.cumsum(chunks(g), axis=-1)
    tri = jnp.tril(jnp.ones((CHUNK, CHUNK), dtype=bool))
    diff = gc[..., :, None] - gc[..., None, :]
    gamma = jnp.where(tri, jnp.exp(jnp.where(tri, diff, 0.0)), 0.0)
    kb = k * beta[..., None]
    a_mat = jnp.tril(jnp.einsum('nbhck,nbhdk->nbhcd', kb, k) * gamma, -1)
    eye = jnp.eye(CHUNK, dtype=f32)
    rhs = jnp.concatenate([v * beta[..., None], kb * jnp.exp(gc)[..., None]], axis=-1)
    sol = lax.linalg.triangular_solve(eye + a_mat, rhs, left_side=True, lower=True, unit_diagonal=True)
    u, w = sol[..., :dv], sol[..., dv:]
    attn = jnp.einsum('nbhck,nbhdk->nbhcd', q, k) * gamma
    q_dec = q * jnp.exp(gc)[..., None]
    g_last = gc[..., -1:]
    k_dec = k * jnp.exp(g_last - gc)[..., None]
    d_last = jnp.exp(g_last)[..., None]

    def step(s, xs):
        qd, kd, u_c, w_c, at, dl = xs
        v_new = u_c - jnp.einsum('bhck,bhkv->bhcv', w_c, s)
        o = jnp.einsum('bhck,bhkv->bhcv', qd, s) + jnp.einsum('bhcd,bhdv->bhcv', at, v_new)
        s = s * dl + jnp.einsum('bhck,bhcv->bhkv', kd, v_new)
        return s, o

    s_fin, o = lax.scan(step, s0.astype(f32), (q_dec, k_dec, u, w, attn, d_last))
    o = jnp.moveaxis(o, (0, 2), (1, 3)).reshape(b_, l_, h_, dv)
    return o, s_fin


def delta_inputs(h, w_conv, a_log_f, dt_f, a_log_b, dt_b):
    b_, l_, _ = h.shape
    qkv = jax.nn.silu(dwconv1d(h[..., :3 * DN], w_conv))
    q, k, v = [t.reshape(b_, l_, NH, HD) for t in jnp.split(qkv, 3, axis=-1)]
    q, k = l2norm(q), l2norm(k)
    gates = h[..., G_OFF:CONF_OFF].astype(jnp.float32)
    a_f, b_f, a_b, b_b = jnp.split(gates, 4, axis=-1)
    g_f = -jnp.exp(a_log_f.astype(jnp.float32)) * jax.nn.softplus(a_f + dt_f.astype(jnp.float32))
    g_b = -jnp.exp(a_log_b.astype(jnp.float32)) * jax.nn.softplus(a_b + dt_b.astype(jnp.float32))
    return q, k, v, g_f, jax.nn.sigmoid(b_f), g_b, jax.nn.sigmoid(b_b)


def bidir_delta(q, k, v, g_f, beta_f, g_b, beta_b, s0_f, s0_b):
    o_f, s_f = gated_delta_chunked(q, k, v, g_f, beta_f, s0_f)
    flip = lambda t: t[:, ::-1]
    o_b, s_b = gated_delta_chunked(flip(q), flip(k), flip(v), flip(g_b), flip(beta_b), s0_b)
    return o_f + flip(o_b), s_f, s_b


def conformer_conv(h, w_dw, b_dw, ln_g, ln_b, grid):
    val = h[..., CONF_OFF:CONF_OFF + CONF]
    gate = h[..., CONF_OFF + CONF:]
    y = val * jax.nn.sigmoid(gate)
    b_, l_, c_ = y.shape
    if grid:
        rows = l_ // GRID_W
        half = c_ // 2
        yh = dwconv1d(y[..., :half].reshape(b_ * rows, GRID_W, half), w_dw[:, :half]).reshape(b_, l_, half)
        yv = y[..., half:].reshape(b_, rows, GRID_W, half).transpose(0, 2, 1, 3).reshape(b_ * GRID_W, rows, half)
        yv = dwconv1d(yv, w_dw[:, half:]).reshape(b_, GRID_W, rows, half).transpose(0, 2, 1, 3).reshape(b_, l_, half)
        y = jnp.concatenate([yh, yv], axis=-1)
    else:
        y = dwconv1d(y, w_dw)
    y = layer_norm(y + b_dw, ln_g, ln_b)
    return jax.nn.silu(y)


def mixer_output(h, o, dn_norm_g, w_dw, b_dw, cln_g, cln_b, w_out, grid):
    b_, l_, _ = h.shape
    ms = jnp.mean(jnp.square(o), axis=-1, keepdims=True)
    o = (o * lax.rsqrt(ms + 1e-6) * dn_norm_g.astype(jnp.float32)).reshape(b_, l_, DN).astype(h.dtype)
    dn_out = o * jax.nn.silu(h[..., Z_OFF:G_OFF])
    conf_out = conformer_conv(h, w_dw, b_dw, cln_g, cln_b, grid)
    return jnp.concatenate([dn_out, conf_out], axis=-1) @ w_out


def sqrelu_mlp(u, w1, b1, w2, b2):
    return jnp.square(jax.nn.relu(u @ w1 + b1)) @ w2 + b2


def setup_inputs(seed: int = 0) -> dict:
    key = jax.random.key(seed)
    ks = jax.random.split(key, 32)
    f32 = jnp.float32
    nrm = lambda k, shape, s: jax.random.normal(k, shape, f32) * s
    beta_dn = (8.0 * DEPTH) ** -0.25
    col_scale = jnp.concatenate([jnp.ones((2 * DN,), f32), jnp.full((DN,), beta_dn, f32),
                                 jnp.ones((IN_COLS - 3 * DN,), f32)])
    dt_f = jnp.exp(jax.random.uniform(ks[12], (DEPTH, NH), f32, math.log(1e-3), math.log(1e-1)))
    dt_b = jnp.exp(jax.random.uniform(ks[13], (DEPTH, NH), f32, math.log(1e-3), math.log(1e-1)))
    return {
        "x": nrm(ks[0], (BATCH, SEQ, D_MODEL), 1.0),
        "c": nrm(ks[1], (BATCH, D_MODEL), 1.0),
        "ctx": nrm(ks[2], (BATCH, CTX_LEN, D_MODEL), 1.0),
        "c_ctx": nrm(ks[3], (D_MODEL,), 1.0),
        "ln_in_g": 1.0 + nrm(ks[4], (D_MODEL,), 0.02),
        "ln_in_b": nrm(ks[5], (D_MODEL,), 0.02),
        "w_mod": nrm(ks[6], (DEPTH, D_MODEL, 6 * D_MODEL), MOD_INIT * D_MODEL ** -0.5),
        "b_mod": nrm(ks[7], (DEPTH, 6 * D_MODEL), 0.02),
        "w_in": nrm(ks[8], (DEPTH, D_MODEL, IN_COLS), D_MODEL ** -0.5) * col_scale,
        "w_qkv_conv": nrm(ks[9], (DEPTH, SHORT_CONV, 3 * DN), SHORT_CONV ** -0.5),
        "a_log_f": jnp.log(jax.random.uniform(ks[10], (DEPTH, NH), f32, 1.0, 16.0)),
        "dt_bias_f": dt_f + jnp.log(-jnp.expm1(-dt_f)),
        "a_log_b": jnp.log(jax.random.uniform(ks[11], (DEPTH, NH), f32, 1.0, 16.0)),
        "dt_bias_b": dt_b + jnp.log(-jnp.expm1(-dt_b)),
        "dn_norm_g": 1.0 + nrm(ks[14], (DEPTH, HD), 0.02),
        "conf_dw_w": nrm(ks[15], (DEPTH, CONF_K, CONF), CONF_K ** -0.5),
        "conf_dw_b": nrm(ks[16], (DEPTH, CONF), 0.02),
        "conf_ln_g": 1.0 + nrm(ks[17], (DEPTH, CONF), 0.02),
        "conf_ln_b": nrm(ks[18], (DEPTH, CONF), 0.02),
        "w_out": nrm(ks[19], (DEPTH, DN + CONF, D_MODEL), beta_dn * (DN + CONF) ** -0.5),
        "ln1_g": 1.0 + nrm(ks[20], (DEPTH, D_MODEL), 0.02),
        "ln1_b": nrm(ks[21], (DEPTH, D_MODEL), 0.02),
        "w_mlp1": nrm(ks[22], (DEPTH, D_MODEL, D_FF), D_MODEL ** -0.5),
        "b_mlp1": nrm(ks[23], (DEPTH, D_FF), 0.02),
        "w_mlp2": nrm(ks[24], (DEPTH, D_FF, D_MODEL), beta_dn * D_FF ** -0.5),
        "b_mlp2": nrm(ks[25], (DEPTH, D_MODEL), 0.02),
        "ln2_g": 1.0 + nrm(ks[26], (DEPTH, D_MODEL), 0.02),
        "ln2_b": nrm(ks[27], (DEPTH, D_MODEL), 0.02),
    }


def reference(x, c, ctx, c_ctx, ln_in_g, ln_in_b, w_mod, b_mod, w_in, w_qkv_conv,
              a_log_f, dt_bias_f, a_log_b, dt_bias_b, dn_norm_g, conf_dw_w, conf_dw_b,
              conf_ln_g, conf_ln_b, w_out, ln1_g, ln1_b, w_mlp1, b_mlp1, w_mlp2, b_mlp2,
              ln2_g, ln2_b):
    alpha = (2.0 * DEPTH) ** 0.25
    b_ = x.shape[0]
    x = layer_norm(x, ln_in_g, ln_in_b)
    xc = layer_norm(ctx, ln_in_g, ln_in_b)
    for l in range(DEPTH):
        last = l == DEPTH - 1
        mod = (jax.nn.silu(c) @ w_mod[l] + b_mod[l])[:, None, :]
        mod_c = (jax.nn.silu(c_ctx) @ w_mod[l] + b_mod[l])[None, None, :]
        sh_a, sc_a, g_a, sh_m, sc_m, g_m = jnp.split(mod, 6, axis=-1)
        csh_a, csc_a, cg_a, csh_m, csc_m, cg_m = jnp.split(mod_c, 6, axis=-1)
        dn_par = (w_qkv_conv[l], a_log_f[l], dt_bias_f[l], a_log_b[l], dt_bias_b[l])
        conf_par = (conf_dw_w[l], conf_dw_b[l], conf_ln_g[l], conf_ln_b[l])

        hc = modulate(xc, csh_a, csc_a) @ w_in[l]
        s0 = jnp.zeros((b_, NH, HD, HD), jnp.float32)
        oc, s_f, s_b = bidir_delta(*delta_inputs(hc, *dn_par), s0, s0)

        h = modulate(x, sh_a, sc_a) @ w_in[l]
        o, _, _ = bidir_delta(*delta_inputs(h, *dn_par), s_f, s_b)
        y = mixer_output(h, o, dn_norm_g[l], *conf_par, w_out[l], True)
        x = layer_norm(alpha * x + g_a * y, ln1_g[l], ln1_b[l])
        y2 = sqrelu_mlp(modulate(x, sh_m, sc_m), w_mlp1[l], b_mlp1[l], w_mlp2[l], b_mlp2[l])
        x = layer_norm(alpha * x + g_m * y2, ln2_g[l], ln2_b[l])

        if not last:
            yc = mixer_output(hc, oc, dn_norm_g[l], *conf_par, w_out[l], False)
            xc = layer_norm(alpha * xc + cg_a * yc, ln1_g[l], ln1_b[l])
            yc2 = sqrelu_mlp(modulate(xc, csh_m, csc_m), w_mlp1[l], b_mlp1[l], w_mlp2[l], b_mlp2[l])
            xc = layer_norm(alpha * xc + cg_m * yc2, ln2_g[l], ln2_b[l])
    return x
```

```python
import functools

import jax
import jax.numpy as jnp
from jax import lax
from jax.experimental import pallas as pl
from jax.experimental.pallas import tpu as pltpu

F32 = jnp.float32
BF16 = jnp.bfloat16

GRID_W = 64
CHUNK = 64
SHORT_CONV = 7
CONF_K = 31
LANES = 128
BF16_ROWS = 16
VMEM_LIMIT = 56 * 1024 * 1024
LN_EPS = 1e-5


def _params(sem):
    return pltpu.CompilerParams(dimension_semantics=sem, vmem_limit_bytes=VMEM_LIMIT)


def _tile(n, preferred):
    t = min(preferred, n)
    while n % t:
        t //= 2
    return t


def _silu(x):
    return x * jax.nn.sigmoid(x)


def _ln(x, g, b):
    mu = jnp.mean(x, axis=-1, keepdims=True)
    xc = x - mu
    var = jnp.mean(xc * xc, axis=-1, keepdims=True)
    return xc * lax.rsqrt(var + LN_EPS) * g + b


def _mod_kernel(c_ref, w_ref, b_ref, o_ref):
    s = _silu(c_ref[...]).astype(BF16)
    o_ref[...] = jnp.dot(s, w_ref[...].astype(BF16), preferred_element_type=F32) + b_ref[...]


def _mod_table(cc, w, b, tn=512):
    r, d = cc.shape
    n = w.shape[1]
    return pl.pallas_call(
        _mod_kernel,
        grid=(n // tn,),
        in_specs=[pl.BlockSpec((r, d), lambda j: (0, 0)),
                  pl.BlockSpec((d, tn), lambda j: (0, j)),
                  pl.BlockSpec((1, tn), lambda j: (0, j))],
        out_specs=pl.BlockSpec((r, tn), lambda j: (0, j)),
        out_shape=jax.ShapeDtypeStruct((r, n), F32),
        compiler_params=_params(("parallel",)),
        name="mod_table",
    )(cc, w, b.reshape(1, n))


def _mod_spec(d, row_of_tile, chunk):
    return pl.BlockSpec((None, 1, d), lambda i: (row_of_tile(i), 0, chunk))


def _ln_mod_kernel(x_ref, g_ref, b_ref, sh_ref, sc_ref, o_ref):
    xn = _ln(x_ref[...], g_ref[...], b_ref[...])
    o_ref[...] = (xn * (1.0 + sc_ref[...]) + sh_ref[...]).astype(o_ref.dtype)


def _ln_mod(x, g, b, mod3, row_of_tile, tl=256):
    m, d = x.shape
    row = pl.BlockSpec((tl, d), lambda i: (i, 0))
    vec = pl.BlockSpec((1, d), lambda i: (0, 0))
    return pl.pallas_call(
        _ln_mod_kernel,
        grid=(m // tl,),
        in_specs=[row, vec, vec, _mod_spec(d, row_of_tile, 0), _mod_spec(d, row_of_tile, 1)],
        out_specs=row,
        out_shape=jax.ShapeDtypeStruct((m, d), BF16),
        compiler_params=_params(("parallel",)),
        name="ln_mod",
    )(x, g.reshape(1, d), b.reshape(1, d), mod3, mod3)


def _x1(x_ref, y_ref, gin_ref, bin_ref, ga_ref, g1_ref, b1_ref, alpha):
    xn = _ln(x_ref[...], gin_ref[...], bin_ref[...])
    return _ln(alpha * xn + ga_ref[...] * y_ref[...].astype(F32), g1_ref[...], b1_ref[...])


def _mlp_in_kernel(x_ref, y_ref, gin_ref, bin_ref, ga_ref, g1_ref, b1_ref, sh_ref, sc_ref, o_ref, *, alpha):
    x1 = _x1(x_ref, y_ref, gin_ref, bin_ref, ga_ref, g1_ref, b1_ref, alpha)
    o_ref[...] = (x1 * (1.0 + sc_ref[...]) + sh_ref[...]).astype(o_ref.dtype)


def _mlp_in(x, y, gin, bin_, g1, b1, mod3, row_of_tile, alpha, tl=256):
    m, d = x.shape
    row = pl.BlockSpec((tl, d), lambda i: (i, 0))
    vec = pl.BlockSpec((1, d), lambda i: (0, 0))
    v = lambda a: a.reshape(1, d)
    return pl.pallas_call(
        functools.partial(_mlp_in_kernel, alpha=alpha),
        grid=(m // tl,),
        in_specs=[row, row, vec, vec, _mod_spec(d, row_of_tile, 2), vec, vec,
                  _mod_spec(d, row_of_tile, 3), _mod_spec(d, row_of_tile, 4)],
        out_specs=row,
        out_shape=jax.ShapeDtypeStruct((m, d), BF16),
        compiler_params=_params(("parallel",)),
        name="mlp_in",
    )(x, y, v(gin), v(bin_), mod3, v(g1), v(b1), mod3, mod3)


def _final_kernel(x_ref, y_ref, y2_ref, gin_ref, bin_ref, ga_ref, g1_ref, b1_ref, bm_ref, gm_ref,
                  g2_ref, b2_ref, o_ref, *, alpha):
    x1 = _x1(x_ref, y_ref, gin_ref, bin_ref, ga_ref, g1_ref, b1_ref, alpha)
    y2 = y2_ref[...].astype(F32) + bm_ref[...]
    o_ref[...] = _ln(alpha * x1 + gm_ref[...] * y2, g2_ref[...], b2_ref[...])


def _final(x, y, y2, gin, bin_, g1, b1, bm, g2, b2, mod3, row_of_tile, alpha, tl=256):
    m, d = x.shape
    row = pl.BlockSpec((tl, d), lambda i: (i, 0))
    vec = pl.BlockSpec((1, d), lambda i: (0, 0))
    v = lambda a: a.reshape(1, d)
    return pl.pallas_call(
        functools.partial(_final_kernel, alpha=alpha),
        grid=(m // tl,),
        in_specs=[row, row, row, vec, vec, _mod_spec(d, row_of_tile, 2), vec, vec, vec,
                  _mod_spec(d, row_of_tile, 5), vec, vec],
        out_specs=row,
        out_shape=jax.ShapeDtypeStruct((m, d), F32),
        compiler_params=_params(("parallel",)),
        name="final_ln",
    )(x, y, y2, v(gin), v(bin_), mod3, v(g1), v(b1), v(bm), mod3, v(g2), v(b2))


def _mm_kernel(a_ref, w_ref, o_ref):
    o_ref[...] = jnp.dot(a_ref[...], w_ref[...], preferred_element_type=F32).astype(o_ref.dtype)


def _mm_relu2_kernel(a_ref, w_ref, b_ref, o_ref):
    h = jnp.dot(a_ref[...], w_ref[...], preferred_element_type=F32) + b_ref[...]
    r = jnp.maximum(h, 0.0)
    o_ref[...] = (r * r).astype(o_ref.dtype)


def _matmul(a, w, out_dtype, bias=None, n_cols=None, tm=1024, tn=1024):
    m, k = a.shape
    n = w.shape[1] if n_cols is None else n_cols
    tm, tn = _tile(m, tm), _tile(n, tn)
    a_spec = pl.BlockSpec((tm, k), lambda i, j: (i, 0))
    w_spec = pl.BlockSpec((k, tn), lambda i, j: (0, j))
    o_spec = pl.BlockSpec((tm, tn), lambda i, j: (i, j))
    if bias is None:
        body, specs, args = _mm_kernel, [a_spec, w_spec], (a, w)
    else:
        body = _mm_relu2_kernel
        specs = [a_spec, w_spec, pl.BlockSpec((1, tn), lambda i, j: (0, j))]
        args = (a, w, bias.reshape(1, -1))
    return pl.pallas_call(
        body,
        grid=(m // tm, n // tn),
        in_specs=specs,
        out_specs=o_spec,
        out_shape=jax.ShapeDtypeStruct((m, n), out_dtype),
        compiler_params=_params(("parallel", "arbitrary")),
        name="matmul",
    )(*args)


def _mm2_kernel(a0_ref, a1_ref, w0_ref, w1_ref, o_ref):
    acc = jnp.dot(a0_ref[...], w0_ref[...], preferred_element_type=F32)
    acc += jnp.dot(a1_ref[...], w1_ref[...], preferred_element_type=F32)
    o_ref[...] = acc.astype(o_ref.dtype)


def _mmk_kernel(a_ref, w_ref, o_ref, acc_ref):
    kk = pl.program_id(2)

    @pl.when(kk == 0)
    def _():
        acc_ref[...] = jnp.zeros_like(acc_ref)

    acc_ref[...] += jnp.dot(a_ref[...], w_ref[...], preferred_element_type=F32)

    @pl.when(kk == pl.num_programs(2) - 1)
    def _():
        o_ref[...] = acc_ref[...].astype(o_ref.dtype)


def _matmul_k(a, w, out_dtype, tm=1024, tn=2048, tk=2048):
    m, k = a.shape
    n = w.shape[1]
    tm, tn, tk = _tile(m, tm), _tile(n, tn), _tile(k, tk)
    return pl.pallas_call(
        _mmk_kernel,
        grid=(m // tm, n // tn, k // tk),
        in_specs=[pl.BlockSpec((tm, tk), lambda i, j, kk: (i, kk)),
                  pl.BlockSpec((tk, tn), lambda i, j, kk: (kk, j))],
        out_specs=pl.BlockSpec((tm, tn), lambda i, j, kk: (i, j)),
        out_shape=jax.ShapeDtypeStruct((m, n), out_dtype),
        scratch_shapes=[pltpu.VMEM((tm, tn), F32)],
        compiler_params=_params(("parallel", "parallel", "arbitrary")),
        name="matmul_k",
    )(a, w)


def _prep_kernel(prev_ref, main_ref, next_ref, w_ref, o_ref, *, tiles_per_seq, norm_blocks):
    i, j = pl.program_id(0), pl.program_id(1)
    t = i % tiles_per_seq
    tl, tc = main_ref.shape
    prev = jnp.where(t > 0, prev_ref[...].astype(F32), 0.0)
    nxt = jnp.where(t < tiles_per_seq - 1, next_ref[...].astype(F32), 0.0)
    xe = jnp.concatenate([prev, main_ref[...].astype(F32), nxt], axis=0)
    w = w_ref[...]
    half = SHORT_CONV // 2
    acc = None
    for tap in range(SHORT_CONV):
        start = BF16_ROWS - half + tap
        term = w[tap:tap + 1, :] * xe[start:start + tl, :]
        acc = term if acc is None else acc + term
    y = _silu(acc)
    for h in range(tc // LANES):
        yh = y[:, h * LANES:(h + 1) * LANES]
        ss = jnp.sum(yh * yh, axis=-1, keepdims=True)
        yn = yh * lax.rsqrt(ss + 1e-6)
        o_ref[:, h * LANES:(h + 1) * LANES] = jnp.where(j < norm_blocks, yn, yh).astype(o_ref.dtype)


def _prep(h_main, w_conv, seq_len, dn, tl=512, tc=512):
    m = h_main.shape[0]
    tl = min(tl, seq_len)
    tiles_per_seq = seq_len // tl
    hb = tl // BF16_ROWS
    last = m // BF16_ROWS - 1
    return pl.pallas_call(
        functools.partial(_prep_kernel, tiles_per_seq=tiles_per_seq, norm_blocks=2 * dn // tc),
        grid=(m // tl, 3 * dn // tc),
        in_specs=[pl.BlockSpec((BF16_ROWS, tc), lambda i, j: (jnp.maximum(i * hb - 1, 0), j)),
                  pl.BlockSpec((tl, tc), lambda i, j: (i, j)),
                  pl.BlockSpec((BF16_ROWS, tc), lambda i, j: (jnp.minimum((i + 1) * hb, last), j)),
                  pl.BlockSpec((SHORT_CONV, tc), lambda i, j: (0, j))],
        out_specs=pl.BlockSpec((tl, tc), lambda i, j: (i, j)),
        out_shape=jax.ShapeDtypeStruct((m, 3 * dn), BF16),
        compiler_params=_params(("parallel", "parallel")),
        name="prep_qkv",
    )(h_main, h_main, h_main, w_conv)


def _cumsum_rows(x, reverse):
    n = x.shape[0]
    row = lax.broadcasted_iota(jnp.int32, x.shape, 0)
    s = 1
    while s < n:
        if reverse:
            x = x + jnp.where(row < n - s, pltpu.roll(x, n - s, axis=0), 0.0)
        else:
            x = x + jnp.where(row >= s, pltpu.roll(x, s, axis=0), 0.0)
        s *= 2
    return x


def _unit_tri_inverse(a):
    n = a.shape[0]
    r = lax.broadcasted_iota(jnp.int32, (n, n), 0)
    c = lax.broadcasted_iota(jnp.int32, (n, n), 1)
    p = jnp.where(r == c, 1.0, 0.0) - a
    apow = a
    steps = max(n.bit_length() - 2, 0)
    for _ in range(steps):
        ab = apow.astype(BF16)
        apow = jnp.dot(ab, ab, preferred_element_type=F32)
        p = p + jnp.dot(p.astype(BF16), apow.astype(BF16), preferred_element_type=F32)
    return p


def _delta_dir(q_ref, k_ref, v_ref, g_ref, par_ref, s_ref, o_ref, *, hb, reverse, scale):
    c_len = q_ref.shape[0]
    graw = g_ref[...]
    z = graw + par_ref[1:2, :]
    softplus = jnp.maximum(z, 0.0) + jnp.log1p(jnp.exp(-jnp.abs(z)))
    gc = _cumsum_rows(-jnp.exp(par_ref[0:1, :]) * softplus, reverse)
    beta = jax.nn.sigmoid(graw)
    g_last = gc[0:1, :] if reverse else gc[c_len - 1:c_len, :]
    e_in = jnp.exp(gc)
    e_out = jnp.exp(g_last - gc)
    d_last = jnp.exp(g_last)
    gc_t = jnp.transpose(jnp.concatenate([gc, jnp.zeros((LANES - c_len, LANES), F32)], axis=0))
    r = lax.broadcasted_iota(jnp.int32, (c_len, c_len), 0)
    c = lax.broadcasted_iota(jnp.int32, (c_len, c_len), 1)
    incl = (r <= c) if reverse else (r >= c)
    strict = (r < c) if reverse else (r > c)
    g_col0 = 2 * hb if reverse else 0
    for h in range(hb):
        cg, cb = g_col0 + h, g_col0 + hb + h
        lanes = slice(h * LANES, (h + 1) * LANES)
        kb16 = k_ref[:, lanes]
        q = q_ref[:, lanes].astype(F32) * scale
        k = kb16.astype(F32)
        v = v_ref[:, lanes].astype(F32)
        b = beta[:, cb:cb + 1]
        ein = e_in[:, cg:cg + 1]
        diff = gc[:, cg:cg + 1] - gc_t[cg:cg + 1, :c_len]
        gamma = jnp.where(incl, jnp.exp(jnp.where(incl, diff, 0.0)), 0.0)
        kbeta = k * b
        m1 = lax.dot_general(jnp.concatenate([kbeta, q], axis=0).astype(BF16), kb16,
                             (((1,), (1,)), ((), ())), preferred_element_type=F32)
        a_mat = jnp.where(strict, m1[:c_len] * gamma, 0.0)
        attn = m1[c_len:] * gamma
        t_inv = _unit_tri_inverse(a_mat)
        rhs = jnp.concatenate([v * b, kbeta * ein], axis=1).astype(BF16)
        uw = jnp.dot(t_inv.astype(BF16), rhs, preferred_element_type=F32)
        u, w = uw[:, :LANES], uw[:, LANES:]
        s = s_ref[h]
        ws = jnp.dot(jnp.concatenate([w, q * ein], axis=0).astype(BF16), s.astype(BF16),
                     preferred_element_type=F32)
        v_new = u - ws[:c_len]
        vn16 = v_new.astype(BF16)
        if o_ref is not None:
            o = ws[c_len:] + jnp.dot(attn.astype(BF16), vn16, preferred_element_type=F32)
            o_ref[:, lanes] = o.astype(o_ref.dtype)
        kd = (k * e_out[:, cg:cg + 1]).astype(BF16)
        s_ref[h] = s * d_last[:, cg:cg + 1] + lax.dot_general(
            kd, vn16, (((0,), (0,)), ((), ())), preferred_element_type=F32)


def _delta_kernel(*refs, hb, scale, write_o):
    (qf, kf, vf, gf, qb, kb, vb, gb, par, s0f, s0b), outs = refs[:11], refs[11:]
    if write_o:
        of, ob, sf, sb = outs
    else:
        (sf, sb), of, ob = outs, None, None

    @pl.when(pl.program_id(2) == 0)
    def _():
        sf[...] = s0f[...]
        sb[...] = s0b[...]

    _delta_dir(qf, kf, vf, gf, par, sf, of, hb=hb, reverse=False, scale=scale)
    _delta_dir(qb, kb, vb, gb, par, sb, ob, hb=hb, reverse=True, scale=scale)


def _delta(qkv, gates, par, s0f, s0b, seq_len, nh, write_o, hb=4):
    m = qkv.shape[0]
    bsz = m // seq_len
    nc = seq_len // CHUNK
    nhb = nh // hb
    w = hb * LANES
    fwd = lambda b, g, c: b * nc + c
    bwd = lambda b, g, c: b * nc + nc - 1 - c

    def qkv_specs(row):
        return [pl.BlockSpec((CHUNK, w), lambda b, g, c, o=o: (row(b, g, c), o * nhb + g)) for o in range(3)]

    g_spec = lambda row: pl.BlockSpec((CHUNK, LANES), lambda b, g, c: (row(b, g, c), g))
    s_spec = pl.BlockSpec((None, hb, LANES, LANES), lambda b, g, c: (b, g, 0, 0))
    o_spec = lambda row: pl.BlockSpec((CHUNK, w), lambda b, g, c: (row(b, g, c), g))
    s_shape = jax.ShapeDtypeStruct((bsz, nh, LANES, LANES), F32)
    o_shape = jax.ShapeDtypeStruct((m, nh * LANES), BF16)
    out_specs = [s_spec, s_spec]
    out_shape = [s_shape, s_shape]
    if write_o:
        out_specs = [o_spec(fwd), o_spec(bwd)] + out_specs
        out_shape = [o_shape, o_shape] + out_shape
    return pl.pallas_call(
        functools.partial(_delta_kernel, hb=hb, scale=float(LANES) ** -0.5, write_o=write_o),
        grid=(bsz, nhb, nc),
        in_specs=qkv_specs(fwd) + [g_spec(fwd)] + qkv_specs(bwd) + [g_spec(bwd)]
        + [pl.BlockSpec((None, 8, LANES), lambda b, g, c: (g, 0, 0)), s_spec, s_spec],
        out_specs=out_specs,
        out_shape=out_shape,
        compiler_params=_params(("parallel", "parallel", "arbitrary")),
        name="delta_rule",
    )(qkv, qkv, qkv, gates, qkv, qkv, qkv, gates, par, s0f, s0b)


def _glu(v_ref, g_ref, rows):
    return v_ref[rows, :].astype(F32) * jax.nn.sigmoid(g_ref[rows, :].astype(F32))


def _conv_h_kernel(v_ref, g_ref, w_ref, o_ref, pad_ref):
    rows, tc = v_ref.shape
    half = CONF_K // 2
    lead = 16
    stride = lead + GRID_W + 16
    w = w_ref[...]
    for g in range(rows // GRID_W):
        base = g * stride
        pad_ref[base:base + lead, :] = jnp.zeros((lead, tc), F32)
        pad_ref[base + lead + GRID_W:base + stride, :] = jnp.zeros((16, tc), F32)
        pad_ref[base + lead:base + lead + GRID_W, :] = _glu(v_ref, g_ref, slice(g * GRID_W, (g + 1) * GRID_W))
    for g in range(rows // GRID_W):
        base = g * stride + lead - half
        for cs in range(tc // LANES):
            lanes = slice(cs * LANES, (cs + 1) * LANES)
            acc = None
            for tap in range(CONF_K):
                term = w[tap:tap + 1, lanes] * pad_ref[base + tap:base + tap + GRID_W, lanes]
                acc = term if acc is None else acc + term
            o_ref[g * GRID_W:(g + 1) * GRID_W, lanes] = acc.astype(o_ref.dtype)


def _conv_h(h_main, w_dw, col_v, col_g, half_c, rows=512, tc=256):
    m = h_main.shape[0]
    ov, og = col_v // tc, col_g // tc
    return pl.pallas_call(
        _conv_h_kernel,
        grid=(m // rows, half_c // tc),
        in_specs=[pl.BlockSpec((rows, tc), lambda i, j: (i, ov + j)),
                  pl.BlockSpec((rows, tc), lambda i, j: (i, og + j)),
                  pl.BlockSpec((CONF_K, tc), lambda i, j: (0, j))],
        out_specs=pl.BlockSpec((rows, tc), lambda i, j: (i, j)),
        out_shape=jax.ShapeDtypeStruct((m, half_c), BF16),
        scratch_shapes=[pltpu.VMEM((rows // GRID_W * (GRID_W + 32), tc), F32)],
        compiler_params=_params(("parallel", "parallel")),
        name="conv_rows",
    )(h_main, h_main, w_dw)


def _conv_v_kernel(v_ref, g_ref, w_ref, o_ref, pad_ref):
    seq, tc = v_ref.shape
    halo = (CONF_K // 2) * GRID_W
    pad_ref[0:halo, :] = jnp.zeros((halo, tc), F32)
    pad_ref[halo + seq:halo + seq + halo, :] = jnp.zeros((halo, tc), F32)
    blk = 512

    def fill(i, carry):
        r0 = pl.multiple_of(i * blk, blk)
        pad_ref[pl.ds(halo + r0, blk), :] = _glu(v_ref, g_ref, pl.ds(r0, blk))
        return carry

    lax.fori_loop(0, seq // blk, fill, 0)
    w = w_ref[...]

    def body(i, carry):
        r0 = pl.multiple_of(i * GRID_W, GRID_W)
        for cs in range(tc // LANES):
            lanes = slice(cs * LANES, (cs + 1) * LANES)
            acc = None
            for tap in range(CONF_K):
                term = w[tap:tap + 1, lanes] * pad_ref[pl.ds(r0 + tap * GRID_W, GRID_W), lanes]
                acc = term if acc is None else acc + term
            o_ref[pl.ds(r0, GRID_W), lanes] = acc.astype(o_ref.dtype)
        return carry

    lax.fori_loop(0, seq // GRID_W, body, 0)


def _conv_v(h_main, w_dw, col_v, col_g, half_c, seq_len, tc=256):
    m = h_main.shape[0]
    ov, og = col_v // tc, col_g // tc
    ow = half_c // tc
    return pl.pallas_call(
        _conv_v_kernel,
        grid=(m // seq_len, half_c // tc),
        in_specs=[pl.BlockSpec((seq_len, tc), lambda b, j: (b, ov + j)),
                  pl.BlockSpec((seq_len, tc), lambda b, j: (b, og + j)),
                  pl.BlockSpec((CONF_K, tc), lambda b, j: (0, ow + j))],
        out_specs=pl.BlockSpec((seq_len, tc), lambda b, j: (b, j)),
        out_shape=jax.ShapeDtypeStruct((m, half_c), BF16),
        scratch_shapes=[pltpu.VMEM((seq_len + 2 * (CONF_K // 2) * GRID_W, tc), F32)],
        compiler_params=_params(("parallel", "parallel")),
        name="conv_cols",
    )(h_main, h_main, w_dw)


def _mix_kernel(of_ref, ob_ref, z_ref, yh_ref, yv_ref, gn_ref, bdw_ref, lg_ref, lb_ref, o_ref):
    dn = of_ref.shape[1]
    for h in range(dn // LANES):
        lanes = slice(h * LANES, (h + 1) * LANES)
        o = of_ref[:, lanes].astype(F32) + ob_ref[:, lanes].astype(F32)
        ms = jnp.mean(o * o, axis=-1, keepdims=True)
        on = o * lax.rsqrt(ms + 1e-6) * gn_ref[...]
        o_ref[:, lanes] = (on * _silu(z_ref[:, lanes].astype(F32))).astype(o_ref.dtype)
    y = jnp.concatenate([yh_ref[...].astype(F32), yv_ref[...].astype(F32)], axis=1) + bdw_ref[...]
    o_ref[:, dn:] = _silu(_ln(y, lg_ref[...], lb_ref[...])).astype(o_ref.dtype)


def _mix(o_f, o_b, h_main, z_blk, y_h, y_v, gn, bdw, lg, lb, tl=256):
    m, dn = o_f.shape
    half_c = y_h.shape[1]
    conf = 2 * half_c
    row = lambda w: pl.BlockSpec((tl, w), lambda i: (i, 0))
    vec = lambda w: pl.BlockSpec((1, w), lambda i: (0, 0))
    return pl.pallas_call(
        _mix_kernel,
        grid=(m // tl,),
        in_specs=[row(dn), row(dn), pl.BlockSpec((tl, dn), lambda i: (i, z_blk)), row(half_c), row(half_c),
                  vec(LANES), vec(conf), vec(conf), vec(conf)],
        out_specs=row(dn + conf),
        out_shape=jax.ShapeDtypeStruct((m, dn + conf), BF16),
        compiler_params=_params(("parallel",)),
        name="mixer_in",
    )(o_f, o_b, h_main, y_h, y_v, gn.reshape(1, LANES), bdw.reshape(1, conf), lg.reshape(1, conf),
      lb.reshape(1, conf))


def kernel(x, c, ctx, c_ctx, ln_in_g, ln_in_b, w_mod, b_mod, w_in, w_qkv_conv, a_log_f, dt_bias_f,
           a_log_b, dt_bias_b, dn_norm_g, conf_dw_w, conf_dw_b, conf_ln_g, conf_ln_b, w_out, ln1_g, ln1_b,
           w_mlp1, b_mlp1, w_mlp2, b_mlp2, ln2_g, ln2_b):
    assert w_mod.shape[0] == 1, "single-layer trunk only"
    bsz, seq, d = x.shape
    ctx_len = ctx.shape[1]
    nh = a_log_f.shape[1]
    dn = d // 2
    conf = d - dn
    assert dn == nh * LANES and seq % GRID_W == 0 and ctx_len % CHUNK == 0
    z_off, g_off, conf_off = 3 * dn, 4 * dn, 4 * dn + 4 * nh
    alpha = 2.0 ** 0.25
    hb = _tile(nh, 4)
    nhb = nh // hb
    m, mc = bsz * seq, bsz * ctx_len

    wi = w_in[0]
    w_main = jnp.concatenate([wi[:, :g_off], wi[:, conf_off:]], axis=1).astype(BF16)
    wg = wi[:, g_off:conf_off].reshape(d, 4, nhb, hb).transpose(0, 2, 1, 3).reshape(d, nhb, 4 * hb)
    wg = jnp.pad(wg, ((0, 0), (0, 0), (0, LANES - 4 * hb))).reshape(d, nhb * LANES).astype(BF16)
    zeros = jnp.zeros_like(a_log_f[0])
    par = jnp.stack([jnp.stack([a_log_f[0], zeros, a_log_b[0], zeros]),
                     jnp.stack([dt_bias_f[0], zeros, dt_bias_b[0], zeros])])
    par = par.reshape(2, 4, nhb, hb).transpose(2, 0, 1, 3).reshape(nhb, 2, 4 * hb)
    par = jnp.pad(par, ((0, 0), (0, 6), (0, LANES - 4 * hb)))
    w_o = w_out[0].astype(BF16)
    w1 = w_mlp1[0].astype(BF16)
    w2 = w_mlp2[0].astype(BF16)

    cc = jnp.concatenate([c, c_ctx[None, :], jnp.zeros((8 - bsz - 1, d), F32)], axis=0)
    mod3 = _mod_table(cc, w_mod[0], b_mod[0]).reshape(8, 1, 6 * d)

    tl = 256
    lat_row = lambda i: (i * tl) // seq
    ctx_row = lambda i: bsz

    uc = _ln_mod(ctx.reshape(mc, d), ln_in_g, ln_in_b, mod3, ctx_row, tl)
    hc = _matmul(uc, w_main, BF16, n_cols=3 * dn)
    gates_c = _matmul(uc, wg, F32, tn=nhb * LANES)
    qkv_c = _prep(hc, w_qkv_conv[0], ctx_len, dn)
    s0 = jnp.zeros((bsz, nh, LANES, LANES), F32)
    s_f, s_b = _delta(qkv_c, gates_c, par, s0, s0, ctx_len, nh, False, hb)

    xr = x.reshape(m, d)
    u0 = _ln_mod(xr, ln_in_g, ln_in_b, mod3, lat_row, tl)
    h_main = _matmul(u0, w_main, BF16)
    gates = _matmul(u0, wg, F32, tn=nhb * LANES)
    qkv = _prep(h_main, w_qkv_conv[0], seq, dn)
    o_f, o_b, _, _ = _delta(qkv, gates, par, s_f, s_b, seq, nh, True, hb)
    y_h = _conv_h(h_main, conf_dw_w[0], g_off, g_off + conf, conf // 2)
    y_v = _conv_v(h_main, conf_dw_w[0], g_off + conf // 2, g_off + conf + conf // 2, conf // 2, seq)
    cat = _mix(o_f, o_b, h_main, z_off // dn, y_h, y_v, dn_norm_g[0], conf_dw_b[0], conf_ln_g[0],
               conf_ln_b[0], tl)
    y = _matmul(cat, w_o, BF16)
    u1 = _mlp_in(xr, y, ln_in_g, ln_in_b, ln1_g[0], ln1_b[0], mod3, lat_row, alpha, tl)
    hid = _matmul(u1, w1, BF16, bias=b_mlp1[0])
    y2 = _matmul_k(hid, w2, BF16)
    out = _final(xr, y, y2, ln_in_g, ln_in_b, ln1_g[0], ln1_b[0], b_mlp2[0], ln2_g[0], ln2_b[0], mod3,
                 lat_row, alpha, tl)
    return out.reshape(bsz, seq, d)
```

```python
import functools

import jax
import jax.numpy as jnp
from jax import lax
from jax.experimental import pallas as pl
from jax.experimental.pallas import tpu as pltpu

F32 = jnp.float32
BF16 = jnp.bfloat16

GRID_W = 64
CHUNK = 64
SHORT_CONV = 7
CONF_K = 31
LANES = 128
BF16_ROWS = 16
VMEM_LIMIT = 56 * 1024 * 1024
LN_EPS = 1e-5


def _params(sem):
    return pltpu.CompilerParams(dimension_semantics=sem, vmem_limit_bytes=VMEM_LIMIT)


def _tile(n, preferred):
    t = min(preferred, n)
    while n % t:
        t //= 2
    return t


def _silu(x):
    return x * jax.nn.sigmoid(x)


def _ln(x, g, b):
    mu = jnp.mean(x, axis=-1, keepdims=True)
    xc = x - mu
    var = jnp.mean(xc * xc, axis=-1, keepdims=True)
    return xc * lax.rsqrt(var + LN_EPS) * g + b


def _mod_kernel(c_ref, w_ref, b_ref, o_ref):
    s = _silu(c_ref[...]).astype(BF16)
    o_ref[...] = jnp.dot(s, w_ref[...].astype(BF16), preferred_element_type=F32) + b_ref[...]


def _mod_table(cc, w, b, tn=512):
    r, d = cc.shape
    n = w.shape[1]
    return pl.pallas_call(
        _mod_kernel,
        grid=(n // tn,),
        in_specs=[pl.BlockSpec((r, d), lambda j: (0, 0)),
                  pl.BlockSpec((d, tn), lambda j: (0, j)),
                  pl.BlockSpec((1, tn), lambda j: (0, j))],
        out_specs=pl.BlockSpec((r, tn), lambda j: (0, j)),
        out_shape=jax.ShapeDtypeStruct((r, n), F32),
        compiler_params=_params(("parallel",)),
        name="mod_table",
    )(cc, w, b.reshape(1, n))


def _mod_spec(d, row_of_tile, chunk):
    return pl.BlockSpec((None, 1, d), lambda i: (row_of_tile(i), 0, chunk))


def _ln_mod_kernel(x_ref, g_ref, b_ref, sh_ref, sc_ref, o_ref):
    xn = _ln(x_ref[...], g_ref[...], b_ref[...])
    o_ref[...] = (xn * (1.0 + sc_ref[...]) + sh_ref[...]).astype(o_ref.dtype)


def _ln_mod(x, g, b, mod3, row_of_tile, tl=256):
    m, d = x.shape
    row = pl.BlockSpec((tl, d), lambda i: (i, 0))
    vec = pl.BlockSpec((1, d), lambda i: (0, 0))
    return pl.pallas_call(
        _ln_mod_kernel,
        grid=(m // tl,),
        in_specs=[row, vec, vec, _mod_spec(d, row_of_tile, 0), _mod_spec(d, row_of_tile, 1)],
        out_specs=row,
        out_shape=jax.ShapeDtypeStruct((m, d), BF16),
        compiler_params=_params(("parallel",)),
        name="ln_mod",
    )(x, g.reshape(1, d), b.reshape(1, d), mod3, mod3)


def _x1(x_ref, y_ref, gin_ref, bin_ref, ga_ref, g1_ref, b1_ref, alpha):
    xn = _ln(x_ref[...], gin_ref[...], bin_ref[...])
    return _ln(alpha * xn + ga_ref[...] * y_ref[...].astype(F32), g1_ref[...], b1_ref[...])


def _mlp_in_kernel(x_ref, y_ref, gin_ref, bin_ref, ga_ref, g1_ref, b1_ref, sh_ref, sc_ref, o_ref, *, alpha):
    x1 = _x1(x_ref, y_ref, gin_ref, bin_ref, ga_ref, g1_ref, b1_ref, alpha)
    o_ref[...] = (x1 * (1.0 + sc_ref[...]) + sh_ref[...]).astype(o_ref.dtype)


def _mlp_in(x, y, gin, bin_, g1, b1, mod3, row_of_tile, alpha, tl=256):
    m, d = x.shape
    row = pl.BlockSpec((tl, d), lambda i: (i, 0))
    vec = pl.BlockSpec((1, d), lambda i: (0, 0))
    v = lambda a: a.reshape(1, d)
    return pl.pallas_call(
        functools.partial(_mlp_in_kernel, alpha=alpha),
        grid=(m // tl,),
        in_specs=[row, row, vec, vec, _mod_spec(d, row_of_tile, 2), vec, vec,
                  _mod_spec(d, row_of_tile, 3), _mod_spec(d, row_of_tile, 4)],
        out_specs=row,
        out_shape=jax.ShapeDtypeStruct((m, d), BF16),
        compiler_params=_params(("parallel",)),
        name="mlp_in",
    )(x, y, v(gin), v(bin_), mod3, v(g1), v(b1), mod3, mod3)


def _final_kernel(x_ref, y_ref, y2_ref, gin_ref, bin_ref, ga_ref, g1_ref, b1_ref, bm_ref, gm_ref,
                  g2_ref, b2_ref, o_ref, *, alpha):
    x1 = _x1(x_ref, y_ref, gin_ref, bin_ref, ga_ref, g1_ref, b1_ref, alpha)
    y2 = y2_ref[...].astype(F32) + bm_ref[...]
    o_ref[...] = _ln(alpha * x1 + gm_ref[...] * y2, g2_ref[...], b2_ref[...])


def _final(x, y, y2, gin, bin_, g1, b1, bm, g2, b2, mod3, row_of_tile, alpha, tl=256):
    m, d = x.shape
    row = pl.BlockSpec((tl, d), lambda i: (i, 0))
    vec = pl.BlockSpec((1, d), lambda i: (0, 0))
    v = lambda a: a.reshape(1, d)
    return pl.pallas_call(
        functools.partial(_final_kernel, alpha=alpha),
        grid=(m // tl,),
        in_specs=[row, row, row, vec, vec, _mod_spec(d, row_of_tile, 2), vec, vec, vec,
                  _mod_spec(d, row_of_tile, 5), vec, vec],
        out_specs=row,
        out_shape=jax.ShapeDtypeStruct((m, d), F32),
        compiler_params=_params(("parallel",)),
        name="final_ln",
    )(x, y, y2, v(gin), v(bin_), mod3, v(g1), v(b1), v(bm), mod3, v(g2), v(b2))


def _mm_kernel(a_ref, w_ref, o_ref):
    o_ref[...] = jnp.dot(a_ref[...], w_ref[...], preferred_element_type=F32).astype(o_ref.dtype)


def _mm_relu2_kernel(a_ref, w_ref, b_ref, o_ref):
    h = jnp.dot(a_ref[...], w_ref[...], preferred_element_type=F32) + b_ref[...]
    r = jnp.maximum(h, 0.0)
    o_ref[...] = (r * r).astype(o_ref.dtype)


def _matmul(a, w, out_dtype, bias=None, n_cols=None, tm=1024, tn=1024):
    m, k = a.shape
    n = w.shape[1] if n_cols is None else n_cols
    tm, tn = _tile(m, tm), _tile(n, tn)
    a_spec = pl.BlockSpec((tm, k), lambda i, j: (i, 0))
    w_spec = pl.BlockSpec((k, tn), lambda i, j: (0, j))
    o_spec = pl.BlockSpec((tm, tn), lambda i, j: (i, j))
    if bias is None:
        body, specs, args = _mm_kernel, [a_spec, w_spec], (a, w)
    else:
        body = _mm_relu2_kernel
        specs = [a_spec, w_spec, pl.BlockSpec((1, tn), lambda i, j: (0, j))]
        args = (a, w, bias.reshape(1, -1))
    return pl.pallas_call(
        body,
        grid=(m // tm, n // tn),
        in_specs=specs,
        out_specs=o_spec,
        out_shape=jax.ShapeDtypeStruct((m, n), out_dtype),
        compiler_params=_params(("parallel", "arbitrary")),
        name="matmul",
    )(*args)


def _mm2_kernel(a0_ref, a1_ref, w0_ref, w1_ref, o_ref):
    acc = jnp.dot(a0_ref[...], w0_ref[...], preferred_element_type=F32)
    acc += jnp.dot(a1_ref[...], w1_ref[...], preferred_element_type=F32)
    o_ref[...] = acc.astype(o_ref.dtype)


def _mmk_kernel(a_ref, w_ref, o_ref, acc_ref):
    kk = pl.program_id(2)

    @pl.when(kk == 0)
    def _():
        acc_ref[...] = jnp.zeros_like(acc_ref)

    acc_ref[...] += jnp.dot(a_ref[...], w_ref[...], preferred_element_type=F32)

    @pl.when(kk == pl.num_programs(2) - 1)
    def _():
        o_ref[...] = acc_ref[...].astype(o_ref.dtype)


def _matmul_k(a, w, out_dtype, tm=1024, tn=2048, tk=2048):
    m, k = a.shape
    n = w.shape[1]
    tm, tn, tk = _tile(m, tm), _tile(n, tn), _tile(k, tk)
    return pl.pallas_call(
        _mmk_kernel,
        grid=(m // tm, n // tn, k // tk),
        in_specs=[pl.BlockSpec((tm, tk), lambda i, j, kk: (i, kk)),
                  pl.BlockSpec((tk, tn), lambda i, j, kk: (kk, j))],
        out_specs=pl.BlockSpec((tm, tn), lambda i, j, kk: (i, j)),
        out_shape=jax.ShapeDtypeStruct((m, n), out_dtype),
        scratch_shapes=[pltpu.VMEM((tm, tn), F32)],
        compiler_params=_params(("parallel", "parallel", "arbitrary")),
        name="matmul_k",
    )(a, w)


def _prep_kernel(prev_ref, main_ref, next_ref, w_ref, o_ref, *, tiles_per_seq, norm_blocks):
    i, j = pl.program_id(0), pl.program_id(1)
    t = i % tiles_per_seq
    tl, tc = main_ref.shape
    prev = jnp.where(t > 0, prev_ref[...].astype(F32), 0.0)
    nxt = jnp.where(t < tiles_per_seq - 1, next_ref[...].astype(F32), 0.0)
    xe = jnp.concatenate([prev, main_ref[...].astype(F32), nxt], axis=0)
    w = w_ref[...]
    half = SHORT_CONV // 2
    acc = None
    for tap in range(SHORT_CONV):
        start = BF16_ROWS - half + tap
        term = w[tap:tap + 1, :] * xe[start:start + tl, :]
        acc = term if acc is None else acc + term
    y = _silu(acc)
    for h in range(tc // LANES):
        yh = y[:, h * LANES:(h + 1) * LANES]
        ss = jnp.sum(yh * yh, axis=-1, keepdims=True)
        yn = yh * lax.rsqrt(ss + 1e-6)
        o_ref[:, h * LANES:(h + 1) * LANES] = jnp.where(j < norm_blocks, yn, yh).astype(o_ref.dtype)


def _prep(h_main, w_conv, seq_len, dn, tl=512, tc=512):
    m = h_main.shape[0]
    tl = min(tl, seq_len)
    tiles_per_seq = seq_len // tl
    hb = tl // BF16_ROWS
    last = m // BF16_ROWS - 1
    return pl.pallas_call(
        functools.partial(_prep_kernel, tiles_per_seq=tiles_per_seq, norm_blocks=2 * dn // tc),
        grid=(m // tl, 3 * dn // tc),
        in_specs=[pl.BlockSpec((BF16_ROWS, tc), lambda i, j: (jnp.maximum(i * hb - 1, 0), j)),
                  pl.BlockSpec((tl, tc), lambda i, j: (i, j)),
                  pl.BlockSpec((BF16_ROWS, tc), lambda i, j: (jnp.minimum((i + 1) * hb, last), j)),
                  pl.BlockSpec((SHORT_CONV, tc), lambda i, j: (0, j))],
        out_specs=pl.BlockSpec((tl, tc), lambda i, j: (i, j)),
        out_shape=jax.ShapeDtypeStruct((m, 3 * dn), BF16),
        compiler_params=_params(("parallel", "parallel")),
        name="prep_qkv",
    )(h_main, h_main, h_main, w_conv)


def _cumsum_rows(x, reverse):
    n = x.shape[0]
    row = lax.broadcasted_iota(jnp.int32, x.shape, 0)
    s = 1
    while s < n:
        if reverse:
            x = x + jnp.where(row < n - s, pltpu.roll(x, n - s, axis=0), 0.0)
        else:
            x = x + jnp.where(row >= s, pltpu.roll(x, s, axis=0), 0.0)
        s *= 2
    return x


def _dot(a, b):
    return jnp.dot(a.astype(BF16), b.astype(BF16), preferred_element_type=F32)


def _delta_units(q_ref, k_ref, v_ref, g_ref, par_ref, s_ref, o_ref, *, hb, reverse, scale):
    c_len = q_ref.shape[0]
    graw = g_ref[...]
    z = graw + par_ref[1:2, :]
    softplus = jnp.maximum(z, 0.0) + jnp.log1p(jnp.exp(-jnp.abs(z)))
    gc = _cumsum_rows(-jnp.exp(par_ref[0:1, :]) * softplus, reverse)
    beta = jax.nn.sigmoid(graw)
    g_last = gc[0:1, :] if reverse else gc[c_len - 1:c_len, :]
    e_in = jnp.exp(gc)
    e_out = jnp.exp(g_last - gc)
    d_last = jnp.exp(g_last)
    gc_t = jnp.transpose(jnp.concatenate([gc, jnp.zeros((LANES - c_len, LANES), F32)], axis=0))
    r = lax.broadcasted_iota(jnp.int32, (c_len, c_len), 0)
    c = lax.broadcasted_iota(jnp.int32, (c_len, c_len), 1)
    incl = (r <= c) if reverse else (r >= c)
    strict = (r < c) if reverse else (r > c)
    g_col0 = 2 * hb if reverse else 0
    units = []
    for h in range(hb):
        cg, cb = g_col0 + h, g_col0 + hb + h
        lanes = slice(h * LANES, (h + 1) * LANES)
        k16 = k_ref[:, lanes]
        q = q_ref[:, lanes].astype(F32) * scale
        k = k16.astype(F32)
        b = beta[:, cb:cb + 1]
        ein = e_in[:, cg:cg + 1]
        diff = gc[:, cg:cg + 1] - gc_t[cg:cg + 1, :c_len]
        kbeta = k * b
        units.append(dict(
            h=h, lanes=lanes, s_ref=s_ref, o_ref=o_ref, strict=strict, k16=k16,
            gamma=jnp.where(incl, jnp.exp(jnp.where(incl, diff, 0.0)), 0.0),
            lhs1=jnp.concatenate([kbeta, q], axis=0).astype(BF16),
            rhs=jnp.concatenate([v_ref[:, lanes].astype(F32) * b, kbeta * ein], axis=1).astype(BF16),
            qd=(q * ein).astype(BF16),
            kd=(k * e_out[:, cg:cg + 1]).astype(BF16),
            dl=d_last[:, cg:cg + 1]))
    return units


def _delta_kernel(*refs, hb, scale, write_o):
    (qf, kf, vf, gf, qb, kb, vb, gb, par, s0f, s0b), outs = refs[:11], refs[11:]
    if write_o:
        of, ob, sf, sb = outs
    else:
        (sf, sb), of, ob = outs, None, None

    @pl.when(pl.program_id(2) == 0)
    def _():
        sf[...] = s0f[...]
        sb[...] = s0b[...]

    units = (_delta_units(qf, kf, vf, gf, par, sf, of, hb=hb, reverse=False, scale=scale)
             + _delta_units(qb, kb, vb, gb, par, sb, ob, hb=hb, reverse=True, scale=scale))
    c_len = qf.shape[0]
    r = lax.broadcasted_iota(jnp.int32, (c_len, c_len), 0)
    c = lax.broadcasted_iota(jnp.int32, (c_len, c_len), 1)
    eye = jnp.where(r == c, 1.0, 0.0)

    for u in units:
        m1 = lax.dot_general(u["lhs1"], u["k16"], (((1,), (1,)), ((), ())), preferred_element_type=F32)
        u["apow"] = jnp.where(u["strict"], m1[:c_len] * u["gamma"], 0.0)
        u["attn"] = (m1[c_len:] * u["gamma"]).astype(BF16)
        u["p"] = eye - u["apow"]
    for _ in range(max(c_len.bit_length() - 2, 0)):
        for u in units:
            u["apow"] = _dot(u["apow"], u["apow"])
        for u in units:
            u["p"] = u["p"] + _dot(u["p"], u["apow"])
    for u in units:
        u["uw"] = _dot(u["p"], u["rhs"])
    for u in units:
        s = u["s_ref"][u["h"]]
        w = u["uw"][:, LANES:].astype(BF16)
        ws = _dot(jnp.concatenate([w, u["qd"]], axis=0), s)
        u["vn"] = (u["uw"][:, :LANES] - ws[:c_len]).astype(BF16)
        u["o"] = ws[c_len:]
        u["s"] = s
    for u in units:
        if u["o_ref"] is not None:
            o = u["o"] + _dot(u["attn"], u["vn"])
            u["o_ref"][:, u["lanes"]] = o.astype(u["o_ref"].dtype)
        u["s_ref"][u["h"]] = u["s"] * u["dl"] + lax.dot_general(
            u["kd"], u["vn"], (((0,), (0,)), ((), ())), preferred_element_type=F32)


def _delta(qkv, gates, par, s0f, s0b, seq_len, nh, write_o, hb=4):
    m = qkv.shape[0]
    bsz = m // seq_len
    nc = seq_len // CHUNK
    nhb = nh // hb
    w = hb * LANES
    fwd = lambda b, g, c: b * nc + c
    bwd = lambda b, g, c: b * nc + nc - 1 - c

    def qkv_specs(row):
        return [pl.BlockSpec((CHUNK, w), lambda b, g, c, o=o: (row(b, g, c), o * nhb + g)) for o in range(3)]

    g_spec = lambda row: pl.BlockSpec((CHUNK, LANES), lambda b, g, c: (row(b, g, c), g))
    s_spec = pl.BlockSpec((None, hb, LANES, LANES), lambda b, g, c: (b, g, 0, 0))
    o_spec = lambda row: pl.BlockSpec((CHUNK, w), lambda b, g, c: (row(b, g, c), g))
    s_shape = jax.ShapeDtypeStruct((bsz, nh, LANES, LANES), F32)
    o_shape = jax.ShapeDtypeStruct((m, nh * LANES), BF16)
    out_specs = [s_spec, s_spec]
    out_shape = [s_shape, s_shape]
    if write_o:
        out_specs = [o_spec(fwd), o_spec(bwd)] + out_specs
        out_shape = [o_shape, o_shape] + out_shape
    return pl.pallas_call(
        functools.partial(_delta_kernel, hb=hb, scale=float(LANES) ** -0.5, write_o=write_o),
        grid=(bsz, nhb, nc),
        in_specs=qkv_specs(fwd) + [g_spec(fwd)] + qkv_specs(bwd) + [g_spec(bwd)]
        + [pl.BlockSpec((None, 8, LANES), lambda b, g, c: (g, 0, 0)), s_spec, s_spec],
        out_specs=out_specs,
        out_shape=out_shape,
        compiler_params=_params(("parallel", "parallel", "arbitrary")),
        name="delta_rule",
    )(qkv, qkv, qkv, gates, qkv, qkv, qkv, gates, par, s0f, s0b)


def _glu(v_ref, g_ref, rows):
    return v_ref[rows, :].astype(F32) * jax.nn.sigmoid(g_ref[rows, :].astype(F32))


def _conv_h_kernel(v_ref, g_ref, w_ref, o_ref, pad_ref):
    rows, tc = v_ref.shape
    half = CONF_K // 2
    lead = 16
    stride = lead + GRID_W + 16
    w = w_ref[...]
    for g in range(rows // GRID_W):
        base = g * stride
        pad_ref[base:base + lead, :] = jnp.zeros((lead, tc), F32)
        pad_ref[base + lead + GRID_W:base + stride, :] = jnp.zeros((16, tc), F32)
        pad_ref[base + lead:base + lead + GRID_W, :] = _glu(v_ref, g_ref, slice(g * GRID_W, (g + 1) * GRID_W))
    for g in range(rows // GRID_W):
        base = g * stride + lead - half
        for cs in range(tc // LANES):
            lanes = slice(cs * LANES, (cs + 1) * LANES)
            acc = None
            for tap in range(CONF_K):
                term = w[tap:tap + 1, lanes] * pad_ref[base + tap:base + tap + GRID_W, lanes]
                acc = term if acc is None else acc + term
            o_ref[g * GRID_W:(g + 1) * GRID_W, lanes] = acc.astype(o_ref.dtype)


def _conv_h(h_main, w_dw, col_v, col_g, half_c, rows=512, tc=256):
    m = h_main.shape[0]
    ov, og = col_v // tc, col_g // tc
    return pl.pallas_call(
        _conv_h_kernel,
        grid=(m // rows, half_c // tc),
        in_specs=[pl.BlockSpec((rows, tc), lambda i, j: (i, ov + j)),
                  pl.BlockSpec((rows, tc), lambda i, j: (i, og + j)),
                  pl.BlockSpec((CONF_K, tc), lambda i, j: (0, j))],
        out_specs=pl.BlockSpec((rows, tc), lambda i, j: (i, j)),
        out_shape=jax.ShapeDtypeStruct((m, half_c), BF16),
        scratch_shapes=[pltpu.VMEM((rows // GRID_W * (GRID_W + 32), tc), F32)],
        compiler_params=_params(("parallel", "parallel")),
        name="conv_rows",
    )(h_main, h_main, w_dw)


def _conv_v_kernel(v_ref, g_ref, w_ref, o_ref, pad_ref):
    seq, tc = v_ref.shape
    halo = (CONF_K // 2) * GRID_W
    pad_ref[0:halo, :] = jnp.zeros((halo, tc), F32)
    pad_ref[halo + seq:halo + seq + halo, :] = jnp.zeros((halo, tc), F32)
    blk = 512

    def fill(i, carry):
        r0 = pl.multiple_of(i * blk, blk)
        pad_ref[pl.ds(halo + r0, blk), :] = _glu(v_ref, g_ref, pl.ds(r0, blk))
        return carry

    lax.fori_loop(0, seq // blk, fill, 0)
    w = w_ref[...]

    def body(i, carry):
        r0 = pl.multiple_of(i * GRID_W, GRID_W)
        for cs in range(tc // LANES):
            lanes = slice(cs * LANES, (cs + 1) * LANES)
            acc = None
            for tap in range(CONF_K):
                term = w[tap:tap + 1, lanes] * pad_ref[pl.ds(r0 + tap * GRID_W, GRID_W), lanes]
                acc = term if acc is None else acc + term
            o_ref[pl.ds(r0, GRID_W), lanes] = acc.astype(o_ref.dtype)
        return carry

    lax.fori_loop(0, seq // GRID_W, body, 0)


def _conv_v(h_main, w_dw, col_v, col_g, half_c, seq_len, tc=256):
    m = h_main.shape[0]
    ov, og = col_v // tc, col_g // tc
    ow = half_c // tc
    return pl.pallas_call(
        _conv_v_kernel,
        grid=(m // seq_len, half_c // tc),
        in_specs=[pl.BlockSpec((seq_len, tc), lambda b, j: (b, ov + j)),
                  pl.BlockSpec((seq_len, tc), lambda b, j: (b, og + j)),
                  pl.BlockSpec((CONF_K, tc), lambda b, j: (0, ow + j))],
        out_specs=pl.BlockSpec((seq_len, tc), lambda b, j: (b, j)),
        out_shape=jax.ShapeDtypeStruct((m, half_c), BF16),
        scratch_shapes=[pltpu.VMEM((seq_len + 2 * (CONF_K // 2) * GRID_W, tc), F32)],
        compiler_params=_params(("parallel", "parallel")),
        name="conv_cols",
    )(h_main, h_main, w_dw)


def _mix_kernel(of_ref, ob_ref, z_ref, yh_ref, yv_ref, gn_ref, bdw_ref, lg_ref, lb_ref, o_ref):
    dn = of_ref.shape[1]
    for h in range(dn // LANES):
        lanes = slice(h * LANES, (h + 1) * LANES)
        o = of_ref[:, lanes].astype(F32) + ob_ref[:, lanes].astype(F32)
        ms = jnp.mean(o * o, axis=-1, keepdims=True)
        on = o * lax.rsqrt(ms + 1e-6) * gn_ref[...]
        o_ref[:, lanes] = (on * _silu(z_ref[:, lanes].astype(F32))).astype(o_ref.dtype)
    y = jnp.concatenate([yh_ref[...].astype(F32), yv_ref[...].astype(F32)], axis=1) + bdw_ref[...]
    o_ref[:, dn:] = _silu(_ln(y, lg_ref[...], lb_ref[...])).astype(o_ref.dtype)


def _mix(o_f, o_b, h_main, z_blk, y_h, y_v, gn, bdw, lg, lb, tl=256):
    m, dn = o_f.shape
    half_c = y_h.shape[1]
    conf = 2 * half_c
    row = lambda w: pl.BlockSpec((tl, w), lambda i: (i, 0))
    vec = lambda w: pl.BlockSpec((1, w), lambda i: (0, 0))
    return pl.pallas_call(
        _mix_kernel,
        grid=(m // tl,),
        in_specs=[row(dn), row(dn), pl.BlockSpec((tl, dn), lambda i: (i, z_blk)), row(half_c), row(half_c),
                  vec(LANES), vec(conf), vec(conf), vec(conf)],
        out_specs=row(dn + conf),
        out_shape=jax.ShapeDtypeStruct((m, dn + conf), BF16),
        compiler_params=_params(("parallel",)),
        name="mixer_in",
    )(o_f, o_b, h_main, y_h, y_v, gn.reshape(1, LANES), bdw.reshape(1, conf), lg.reshape(1, conf),
      lb.reshape(1, conf))


def kernel(x, c, ctx, c_ctx, ln_in_g, ln_in_b, w_mod, b_mod, w_in, w_qkv_conv, a_log_f, dt_bias_f,
           a_log_b, dt_bias_b, dn_norm_g, conf_dw_w, conf_dw_b, conf_ln_g, conf_ln_b, w_out, ln1_g, ln1_b,
           w_mlp1, b_mlp1, w_mlp2, b_mlp2, ln2_g, ln2_b):
    assert w_mod.shape[0] == 1, "single-layer trunk only"
    bsz, seq, d = x.shape
    ctx_len = ctx.shape[1]
    nh = a_log_f.shape[1]
    dn = d // 2
    conf = d - dn
    assert dn == nh * LANES and seq % GRID_W == 0 and ctx_len % CHUNK == 0
    z_off, g_off, conf_off = 3 * dn, 4 * dn, 4 * dn + 4 * nh
    alpha = 2.0 ** 0.25
    hb = _tile(nh, 8)
    nhb = nh // hb
    m, mc = bsz * seq, bsz * ctx_len

    wi = w_in[0]
    w_main = jnp.concatenate([wi[:, :g_off], wi[:, conf_off:]], axis=1).astype(BF16)
    wg = wi[:, g_off:conf_off].reshape(d, 4, nhb, hb).transpose(0, 2, 1, 3).reshape(d, nhb, 4 * hb)
    wg = jnp.pad(wg, ((0, 0), (0, 0), (0, LANES - 4 * hb))).reshape(d, nhb * LANES).astype(BF16)
    zeros = jnp.zeros_like(a_log_f[0])
    par = jnp.stack([jnp.stack([a_log_f[0], zeros, a_log_b[0], zeros]),
                     jnp.stack([dt_bias_f[0], zeros, dt_bias_b[0], zeros])])
    par = par.reshape(2, 4, nhb, hb).transpose(2, 0, 1, 3).reshape(nhb, 2, 4 * hb)
    par = jnp.pad(par, ((0, 0), (0, 6), (0, LANES - 4 * hb)))
    w_o = w_out[0].astype(BF16)
    w1 = w_mlp1[0].astype(BF16)
    w2 = w_mlp2[0].astype(BF16)

    cc = jnp.concatenate([c, c_ctx[None, :], jnp.zeros((8 - bsz - 1, d), F32)], axis=0)
    mod3 = _mod_table(cc, w_mod[0], b_mod[0]).reshape(8, 1, 6 * d)

    tl = 256
    lat_row = lambda i: (i * tl) // seq
    ctx_row = lambda i: bsz

    uc = _ln_mod(ctx.reshape(mc, d), ln_in_g, ln_in_b, mod3, ctx_row, tl)
    hc = _matmul(uc, w_main, BF16, n_cols=3 * dn)
    gates_c = _matmul(uc, wg, F32, tn=nhb * LANES)
    qkv_c = _prep(hc, w_qkv_conv[0], ctx_len, dn)
    s0 = jnp.zeros((bsz, nh, LANES, LANES), F32)
    s_f, s_b = _delta(qkv_c, gates_c, par, s0, s0, ctx_len, nh, False, hb)

    xr = x.reshape(m, d)
    u0 = _ln_mod(xr, ln_in_g, ln_in_b, mod3, lat_row, tl)
    h_main = _matmul(u0, w_main, BF16)
    gates = _matmul(u0, wg, F32, tn=nhb * LANES)
    qkv = _prep(h_main, w_qkv_conv[0], seq, dn)
    o_f, o_b, _, _ = _delta(qkv, gates, par, s_f, s_b, seq, nh, True, hb)
    y_h = _conv_h(h_main, conf_dw_w[0], g_off, g_off + conf, conf // 2)
    y_v = _conv_v(h_main, conf_dw_w[0], g_off + conf // 2, g_off + conf + conf // 2, conf // 2, seq)
    cat = _mix(o_f, o_b, h_main, z_off // dn, y_h, y_v, dn_norm_g[0], conf_dw_b[0], conf_ln_g[0],
               conf_ln_b[0], tl)
    y = _matmul(cat, w_o, BF16)
    u1 = _mlp_in(xr, y, ln_in_g, ln_in_b, ln1_g[0], ln1_b[0], mod3, lat_row, alpha, tl)
    hid = _matmul(u1, w1, BF16, bias=b_mlp1[0])
    y2 = _matmul_k(hid, w2, BF16)
    out = _final(xr, y, y2, ln_in_g, ln_in_b, ln1_g[0], ln1_b[0], b_mlp2[0], ln2_g[0], ln2_b[0], mod3,
                 lat_row, alpha, tl)
    return out.reshape(bsz, seq, d)
```

```python
import functools

import jax
import jax.numpy as jnp
from jax import lax
from jax.experimental import pallas as pl
from jax.experimental.pallas import tpu as pltpu

F32 = jnp.float32
BF16 = jnp.bfloat16

GRID_W = 64
CHUNK = 64
SHORT_CONV = 7
CONF_K = 31
LANES = 128
SUBLANES = 8
BF16_ROWS = 16
VMEM_LIMIT = 56 * 1024 * 1024
LN_EPS = 1e-5


def _params(sem):
    return pltpu.CompilerParams(dimension_semantics=sem, vmem_limit_bytes=VMEM_LIMIT)


def _tile(n, preferred):
    t = min(preferred, n)
    while n % t:
        t //= 2
    return t


def _silu(x):
    return x * jax.nn.sigmoid(x)


def _ln(x, g, b):
    mu = jnp.mean(x, axis=-1, keepdims=True)
    xc = x - mu
    var = jnp.mean(xc * xc, axis=-1, keepdims=True)
    return xc * lax.rsqrt(var + LN_EPS) * g + b


def _mod_kernel(c_ref, w_ref, b_ref, o_ref):
    s = _silu(c_ref[...]).astype(BF16)
    o_ref[...] = jnp.dot(s, w_ref[...].astype(BF16), preferred_element_type=F32) + b_ref[...]


def _mod_table(cc, w, b, tn=512):
    r, d = cc.shape
    n = w.shape[1]
    return pl.pallas_call(
        _mod_kernel,
        grid=(n // tn,),
        in_specs=[pl.BlockSpec((r, d), lambda j: (0, 0)),
                  pl.BlockSpec((d, tn), lambda j: (0, j)),
                  pl.BlockSpec((1, tn), lambda j: (0, j))],
        out_specs=pl.BlockSpec((r, tn), lambda j: (0, j)),
        out_shape=jax.ShapeDtypeStruct((r, n), F32),
        compiler_params=_params(("parallel",)),
        name="mod_table",
    )(cc, w, b.reshape(1, n))


def _mod_spec(d, row_of_tile, chunk):
    return pl.BlockSpec((None, 1, d), lambda i: (row_of_tile(i), 0, chunk))


def _ln_mod_kernel(x_ref, g_ref, b_ref, sh_ref, sc_ref, o_ref):
    xn = _ln(x_ref[...], g_ref[...], b_ref[...])
    o_ref[...] = (xn * (1.0 + sc_ref[...]) + sh_ref[...]).astype(o_ref.dtype)


def _ln_mod(x, g, b, mod3, row_of_tile, tl=256):
    m, d = x.shape
    row = pl.BlockSpec((tl, d), lambda i: (i, 0))
    vec = pl.BlockSpec((1, d), lambda i: (0, 0))
    return pl.pallas_call(
        _ln_mod_kernel,
        grid=(m // tl,),
        in_specs=[row, vec, vec, _mod_spec(d, row_of_tile, 0), _mod_spec(d, row_of_tile, 1)],
        out_specs=row,
        out_shape=jax.ShapeDtypeStruct((m, d), BF16),
        compiler_params=_params(("parallel",)),
        name="ln_mod",
    )(x, g.reshape(1, d), b.reshape(1, d), mod3, mod3)


def _mlp_in_kernel(x_ref, y_ref, gin_ref, bin_ref, ga_ref, g1_ref, b1_ref, sh_ref, sc_ref, u_ref, x1_ref,
                   *, alpha):
    xn = _ln(x_ref[...], gin_ref[...], bin_ref[...])
    x1 = _ln(alpha * xn + ga_ref[...] * y_ref[...].astype(F32), g1_ref[...], b1_ref[...])
    x1_ref[...] = x1
    u_ref[...] = (x1 * (1.0 + sc_ref[...]) + sh_ref[...]).astype(u_ref.dtype)


def _mlp_in(x, y, gin, bin_, g1, b1, mod3, row_of_tile, alpha, tl=256):
    m, d = x.shape
    row = pl.BlockSpec((tl, d), lambda i: (i, 0))
    vec = pl.BlockSpec((1, d), lambda i: (0, 0))
    v = lambda a: a.reshape(1, d)
    return pl.pallas_call(
        functools.partial(_mlp_in_kernel, alpha=alpha),
        grid=(m // tl,),
        in_specs=[row, row, vec, vec, _mod_spec(d, row_of_tile, 2), vec, vec,
                  _mod_spec(d, row_of_tile, 3), _mod_spec(d, row_of_tile, 4)],
        out_specs=[row, row],
        out_shape=[jax.ShapeDtypeStruct((m, d), BF16), jax.ShapeDtypeStruct((m, d), F32)],
        compiler_params=_params(("parallel",)),
        name="mlp_in",
    )(x, y, v(gin), v(bin_), mod3, v(g1), v(b1), mod3, mod3)


def _final_kernel(x1_ref, y2_ref, bm_ref, gm_ref, g2_ref, b2_ref, o_ref, *, alpha):
    y2 = y2_ref[...].astype(F32) + bm_ref[...]
    o_ref[...] = _ln(alpha * x1_ref[...] + gm_ref[...] * y2, g2_ref[...], b2_ref[...])


def _final(x1, y2, bm, g2, b2, mod3, row_of_tile, alpha, tl=256):
    m, d = x1.shape
    row = pl.BlockSpec((tl, d), lambda i: (i, 0))
    vec = pl.BlockSpec((1, d), lambda i: (0, 0))
    v = lambda a: a.reshape(1, d)
    return pl.pallas_call(
        functools.partial(_final_kernel, alpha=alpha),
        grid=(m // tl,),
        in_specs=[row, row, vec, _mod_spec(d, row_of_tile, 5), vec, vec],
        out_specs=row,
        out_shape=jax.ShapeDtypeStruct((m, d), F32),
        compiler_params=_params(("parallel",)),
        name="final_ln",
    )(x1, y2, v(bm), mod3, v(g2), v(b2))


def _mm_kernel(a_ref, w_ref, o_ref):
    o_ref[...] = jnp.dot(a_ref[...], w_ref[...], preferred_element_type=F32).astype(o_ref.dtype)


def _matmul(a, w, out_dtype, tm=1024, tn=1024):
    m, k = a.shape
    n = w.shape[1]
    tm, tn = _tile(m, tm), _tile(n, tn)
    return pl.pallas_call(
        _mm_kernel,
        grid=(m // tm, n // tn),
        in_specs=[pl.BlockSpec((tm, k), lambda i, j: (i, 0)), pl.BlockSpec((k, tn), lambda i, j: (0, j))],
        out_specs=pl.BlockSpec((tm, tn), lambda i, j: (i, j)),
        out_shape=jax.ShapeDtypeStruct((m, n), out_dtype),
        compiler_params=_params(("parallel", "arbitrary")),
        name="matmul",
    )(a, w)


def _mm_wres_kernel(a_ref, w_ref, *rest, relu2):
    if relu2:
        b_ref, o_ref, w16_ref = rest
    else:
        o_ref, w16_ref = rest

    @pl.when(pl.program_id(1) == 0)
    def _():
        w16_ref[...] = w_ref[...].astype(BF16)

    h = jnp.dot(a_ref[...], w16_ref[...], preferred_element_type=F32)
    if relu2:
        r = jnp.maximum(h + b_ref[...], 0.0)
        h = r * r
    o_ref[...] = h.astype(o_ref.dtype)


def _matmul_wres(a, w3, out_dtype, n_cols, bias=None, tm=1024, tn=1024):
    m, k = a.shape
    tm, tn = _tile(m, tm), _tile(n_cols, tn)
    specs = [pl.BlockSpec((tm, k), lambda j, i: (i, 0)),
             pl.BlockSpec((None, k, tn), lambda j, i: (0, 0, j), pipeline_mode=pl.Buffered(1))]
    args = (a, w3)
    if bias is not None:
        specs.append(pl.BlockSpec((1, tn), lambda j, i: (0, j)))
        args += (bias.reshape(1, -1),)
    return pl.pallas_call(
        functools.partial(_mm_wres_kernel, relu2=bias is not None),
        grid=(n_cols // tn, m // tm),
        in_specs=specs,
        out_specs=pl.BlockSpec((tm, tn), lambda j, i: (i, j)),
        out_shape=jax.ShapeDtypeStruct((m, n_cols), out_dtype),
        scratch_shapes=[pltpu.VMEM((k, tn), BF16)],
        compiler_params=_params(("arbitrary", "arbitrary")),
        name="matmul_wres",
    )(*args)


def _mmk_kernel(a_ref, w_ref, o_ref, acc_ref):
    kk = pl.program_id(2)

    @pl.when(kk == 0)
    def _():
        acc_ref[...] = jnp.zeros_like(acc_ref)

    acc_ref[...] += jnp.dot(a_ref[...], w_ref[...], preferred_element_type=F32)

    @pl.when(kk == pl.num_programs(2) - 1)
    def _():
        o_ref[...] = acc_ref[...].astype(o_ref.dtype)


def _matmul_k(a, w, out_dtype, tm=1024, tn=2048, tk=2048):
    m, k = a.shape
    n = w.shape[1]
    tm, tn, tk = _tile(m, tm), _tile(n, tn), _tile(k, tk)
    return pl.pallas_call(
        _mmk_kernel,
        grid=(m // tm, n // tn, k // tk),
        in_specs=[pl.BlockSpec((tm, tk), lambda i, j, kk: (i, kk)),
                  pl.BlockSpec((tk, tn), lambda i, j, kk: (kk, j))],
        out_specs=pl.BlockSpec((tm, tn), lambda i, j, kk: (i, j)),
        out_shape=jax.ShapeDtypeStruct((m, n), out_dtype),
        scratch_shapes=[pltpu.VMEM((tm, tn), F32)],
        compiler_params=_params(("parallel", "parallel", "arbitrary")),
        name="matmul_k",
    )(a, w)


def _shift_matrix(blk, win):
    half = SHORT_CONV // 2
    side = [tap for tap in range(SHORT_CONV) if tap != half]
    r = jnp.arange(blk)[:, None]
    c = jnp.arange(win)[None, :]
    return jnp.concatenate([(c == r + BF16_ROWS - half + tap) for tap in side], axis=0).astype(BF16)


def _prep_kernel(prev_ref, main_ref, next_ref, w_ref, shift_ref, o_ref, *, tiles_per_seq, norm_blocks):
    i, j = pl.program_id(0), pl.program_id(1)
    t = i % tiles_per_seq
    tl, tc = main_ref.shape
    half = SHORT_CONV // 2
    blk = min(LANES, tl)
    win = 2 * blk
    prev = jnp.where(t > 0, prev_ref[...], 0.0)
    nxt = jnp.where(t < tiles_per_seq - 1, next_ref[...], 0.0)
    tail = jnp.zeros((win - blk - 2 * BF16_ROWS, tc), BF16)
    ext = jnp.concatenate([prev, main_ref[...], nxt, tail], axis=0)
    side = [tap for tap in range(SHORT_CONV) if tap != half]
    shift = shift_ref[...]
    w = w_ref[...]
    for b in range(tl // blk):
        rows = slice(b * blk, (b + 1) * blk)
        moved = jnp.dot(shift, ext[b * blk:b * blk + win, :], preferred_element_type=F32)
        acc = w[half:half + 1, :] * main_ref[rows, :].astype(F32)
        for n, tap in enumerate(side):
            acc = acc + w[tap:tap + 1, :] * moved[n * blk:(n + 1) * blk, :]
        y = _silu(acc)
        for h in range(tc // LANES):
            lanes = slice(h * LANES, (h + 1) * LANES)
            yh = y[:, lanes]
            ss = jnp.sum(yh * yh, axis=-1, keepdims=True)
            yn = yh * lax.rsqrt(ss + 1e-6)
            o_ref[rows, lanes] = jnp.where(j < norm_blocks, yn, yh).astype(o_ref.dtype)


def _prep(h_main, w_conv, seq_len, dn, tl=512, tc=512):
    m = h_main.shape[0]
    tl = min(tl, seq_len)
    tiles_per_seq = seq_len // tl
    hb = tl // BF16_ROWS
    last = m // BF16_ROWS - 1
    blk = min(LANES, tl)
    shift = _shift_matrix(blk, 2 * blk)
    return pl.pallas_call(
        functools.partial(_prep_kernel, tiles_per_seq=tiles_per_seq, norm_blocks=2 * dn // tc),
        grid=(m // tl, 3 * dn // tc),
        in_specs=[pl.BlockSpec((BF16_ROWS, tc), lambda i, j: (jnp.maximum(i * hb - 1, 0), j)),
                  pl.BlockSpec((tl, tc), lambda i, j: (i, j)),
                  pl.BlockSpec((BF16_ROWS, tc), lambda i, j: (jnp.minimum((i + 1) * hb, last), j)),
                  pl.BlockSpec((SHORT_CONV, tc), lambda i, j: (0, j)),
                  pl.BlockSpec(shift.shape, lambda i, j: (0, 0))],
        out_specs=pl.BlockSpec((tl, tc), lambda i, j: (i, j)),
        out_shape=jax.ShapeDtypeStruct((m, 3 * dn), BF16),
        compiler_params=_params(("parallel", "parallel")),
        name="prep_qkv",
    )(h_main, h_main, h_main, w_conv, shift)


def _cumsum_rows(x, reverse):
    n = x.shape[0]
    row = lax.broadcasted_iota(jnp.int32, x.shape, 0)
    s = 1
    while s < n:
        if reverse:
            x = x + jnp.where(row < n - s, pltpu.roll(x, n - s, axis=0), 0.0)
        else:
            x = x + jnp.where(row >= s, pltpu.roll(x, s, axis=0), 0.0)
        s *= 2
    return x


def _dot(a, b):
    return jnp.dot(a.astype(BF16), b.astype(BF16), preferred_element_type=F32)


def _delta_units(q_ref, k_ref, v_ref, g_ref, par_ref, s_ref, o_ref, *, rows, hb, reverse, scale):
    c_len = rows.stop - rows.start
    graw = g_ref[rows, :]
    z = graw + par_ref[1:2, :]
    softplus = jnp.maximum(z, 0.0) + jnp.log1p(jnp.exp(-jnp.abs(z)))
    gc = _cumsum_rows(-jnp.exp(par_ref[0:1, :]) * softplus, reverse)
    beta = jax.nn.sigmoid(graw)
    g_last = gc[0:1, :] if reverse else gc[c_len - 1:c_len, :]
    e_in = jnp.exp(gc)
    e_out = jnp.exp(g_last - gc)
    d_last = jnp.exp(g_last)
    gc_t = jnp.transpose(jnp.concatenate([gc, gc], axis=0))
    row = lax.broadcasted_iota(jnp.int32, (c_len, LANES), 0)
    lane = lax.broadcasted_iota(jnp.int32, (c_len, LANES), 1)
    second = lane >= c_len
    col = jnp.where(second, lane - c_len, lane)
    incl = (row <= col) if reverse else (row >= col)
    strict = (row < col) if reverse else (row > col)
    g_col0 = 2 * hb if reverse else 0
    zeros = jnp.zeros((c_len, LANES), BF16)

    def per_head(t, c0, c1, rows):
        return jnp.concatenate([jnp.broadcast_to(t[:, c0:c0 + 1], (rows, LANES)),
                                jnp.broadcast_to(t[:, c1:c1 + 1], (rows, LANES))], axis=1)

    pairs = []
    for p in range(hb // 2):
        cg0, cg1 = g_col0 + 2 * p, g_col0 + 2 * p + 1
        lanes = slice(2 * p * LANES, (2 * p + 2) * LANES)
        k16 = k_ref[rows, lanes]
        q = q_ref[rows, lanes].astype(F32) * scale
        k = k16.astype(F32)
        b = per_head(beta, cg0 + hb, cg1 + hb, c_len)
        ein = per_head(e_in, cg0, cg1, c_len)
        diff = (jnp.where(second, gc[:, cg1:cg1 + 1], gc[:, cg0:cg0 + 1])
                - jnp.where(second[0:1], gc_t[cg1:cg1 + 1, :], gc_t[cg0:cg0 + 1, :]))
        kbeta = k * b
        vb = (v_ref[rows, lanes].astype(F32) * b).astype(BF16)
        kbe = (kbeta * ein).astype(BF16)
        pairs.append(dict(
            pi=p, rows=rows, lanes=lanes, s_ref=s_ref, o_ref=o_ref, strict=strict,
            eye=jnp.where(row == col, 1.0, 0.0),
            gamma=jnp.where(incl, jnp.exp(jnp.where(incl, diff, 0.0)), 0.0),
            lhs1=jnp.concatenate([kbeta, q], axis=0).astype(BF16),
            k_bd=jnp.concatenate([jnp.concatenate([k16[:, :LANES], zeros], axis=1),
                                  jnp.concatenate([zeros, k16[:, LANES:]], axis=1)], axis=0),
            rhs_bd=jnp.concatenate(
                [jnp.concatenate([vb[:, :LANES], zeros, kbe[:, :LANES], zeros], axis=1),
                 jnp.concatenate([zeros, vb[:, LANES:], zeros, kbe[:, LANES:]], axis=1)], axis=0),
            qd=(q * ein).astype(BF16),
            kd=k * per_head(e_out, cg0, cg1, c_len),
            dl=per_head(d_last, cg0, cg1, 1)))
    return pairs


def _same_block_mask(width, c_len):
    shift = c_len.bit_length() - 1
    r = lax.broadcasted_iota(jnp.int32, (width, width), 0)
    c = lax.broadcasted_iota(jnp.int32, (width, width), 1)
    return lax.shift_right_logical(r, shift) == lax.shift_right_logical(c, shift)


def _block_diag(x16, same):
    n = x16.shape[1] // x16.shape[0]
    return jnp.where(same, jnp.concatenate([x16] * n, axis=0), 0.0)


def _delta_kernel(*refs, hb, scale, write_o):
    (qf, kf, vf, gf, qb, kb, vb, gb, par, s0f, s0b), outs = refs[:11], refs[11:]
    if write_o:
        of, ob, sf, sb = outs
    else:
        (sf, sb), of, ob = outs, None, None

    @pl.when(pl.program_id(2) == 0)
    def _():
        sf[...] = s0f[...]
        sb[...] = s0b[...]

    c_len = CHUNK
    n_sub = qf.shape[0] // c_len
    phases = []
    for j in range(n_sub):
        rf = slice(j * c_len, (j + 1) * c_len)
        rb = slice((n_sub - 1 - j) * c_len, (n_sub - j) * c_len)
        phases.append(_delta_units(qf, kf, vf, gf, par, sf, of, rows=rf, hb=hb, reverse=False, scale=scale)
                      + _delta_units(qb, kb, vb, gb, par, sb, ob, rows=rb, hb=hb, reverse=True, scale=scale))
    units = [u for ph in phases for u in ph]
    dn_t = (((1,), (1,)), ((), ()))

    for u in units:
        m1 = lax.dot_general(u["lhs1"], u["k_bd"], dn_t, preferred_element_type=F32)
        u["a"] = jnp.where(u["strict"], m1[:c_len] * u["gamma"], 0.0)
        u["attn"] = (m1[c_len:] * u["gamma"]).astype(BF16)
    quads = [dict(pairs=units[i:i + 2]) for i in range(0, len(units), 2)]
    masks = {}
    for qd in quads:
        a = jnp.concatenate([u["a"] for u in qd["pairs"]], axis=1)
        qd["same"] = masks.setdefault(a.shape[1], _same_block_mask(a.shape[1], c_len))
        qd["p"] = jnp.concatenate([u["eye"] for u in qd["pairs"]], axis=1) - a
        a16 = a.astype(BF16)
        qd["apow"] = jnp.dot(a16, _block_diag(a16, qd["same"]), preferred_element_type=F32).astype(BF16)
    rounds = max(c_len.bit_length() - 2, 0)
    for i in range(rounds):
        for qd in quads:
            bd = _block_diag(qd["apow"], qd["same"])
            if i + 1 < rounds:
                res = jnp.dot(jnp.concatenate([qd["p"].astype(BF16), qd["apow"]], axis=0), bd,
                              preferred_element_type=F32)
                qd["p"] = qd["p"] + res[:c_len]
                qd["apow"] = res[c_len:].astype(BF16)
            else:
                qd["p"] = qd["p"] + jnp.dot(qd["p"].astype(BF16), bd, preferred_element_type=F32)
    for qd in quads:
        for j, u in enumerate(qd["pairs"]):
            t16 = qd["p"][:, 2 * j * c_len:2 * (j + 1) * c_len].astype(BF16)
            u["uw"] = jnp.dot(t16, u["rhs_bd"], preferred_element_type=F32)
    zeros = jnp.zeros((LANES, LANES), BF16)
    zc = jnp.zeros((c_len, LANES), BF16)
    for u in units:
        u["kd_t"] = jnp.transpose(
            jnp.concatenate([u["kd"][:, :LANES], u["kd"][:, LANES:]], axis=0)).astype(BF16)
    for phase in phases:
        for u in phase:
            s = u["s_ref"][u["pi"]]
            s16 = s.astype(BF16)
            s_bd = jnp.concatenate([jnp.concatenate([s16[:, :LANES], zeros], axis=1),
                                    jnp.concatenate([zeros, s16[:, LANES:]], axis=1)], axis=0)
            w = u["uw"][:, 2 * LANES:].astype(BF16)
            ws = jnp.dot(jnp.concatenate([w, u["qd"]], axis=0), s_bd, preferred_element_type=F32)
            vn = (u["uw"][:, :2 * LANES] - ws[:c_len]).astype(BF16)
            u["vn_bd"] = jnp.concatenate([jnp.concatenate([vn[:, :LANES], zc], axis=1),
                                          jnp.concatenate([zc, vn[:, LANES:]], axis=1)], axis=0)
            u["o"] = ws[c_len:]
            u["s"] = s
        for u in phase:
            if u["o_ref"] is None:
                ds = jnp.dot(u["kd_t"], u["vn_bd"], preferred_element_type=F32)
            else:
                res = jnp.dot(jnp.concatenate([u["attn"], u["kd_t"]], axis=0), u["vn_bd"],
                              preferred_element_type=F32)
                u["o_ref"][u["rows"], u["lanes"]] = (u["o"] + res[:c_len]).astype(u["o_ref"].dtype)
                ds = res[c_len:]
            u["s_ref"][u["pi"]] = u["s"] * u["dl"] + ds


def _delta(qkv, gates, par, s0f, s0b, seq_len, nh, write_o, hb=4, rows=2 * CHUNK):
    m = qkv.shape[0]
    bsz = m // seq_len
    assert hb % 4 == 0 and seq_len % rows == 0 and rows % CHUNK == 0
    nc = seq_len // rows
    nhb = nh // hb
    w = hb * LANES
    fwd = lambda b, g, c: b * nc + c
    bwd = lambda b, g, c: b * nc + nc - 1 - c

    def qkv_specs(row):
        return [pl.BlockSpec((rows, w), lambda b, g, c, o=o: (row(b, g, c), o * nhb + g)) for o in range(3)]

    g_spec = lambda row: pl.BlockSpec((rows, LANES), lambda b, g, c: (row(b, g, c), g))
    s_spec = pl.BlockSpec((None, hb // 2, LANES, 2 * LANES), lambda b, g, c: (b, g, 0, 0))
    o_spec = lambda row: pl.BlockSpec((rows, w), lambda b, g, c: (row(b, g, c), g))
    s_shape = jax.ShapeDtypeStruct((bsz, nh // 2, LANES, 2 * LANES), F32)
    o_shape = jax.ShapeDtypeStruct((m, nh * LANES), BF16)
    out_specs = [s_spec, s_spec]
    out_shape = [s_shape, s_shape]
    if write_o:
        out_specs = [o_spec(fwd), o_spec(bwd)] + out_specs
        out_shape = [o_shape, o_shape] + out_shape
    return pl.pallas_call(
        functools.partial(_delta_kernel, hb=hb, scale=float(LANES) ** -0.5, write_o=write_o),
        grid=(bsz, nhb, nc),
        in_specs=qkv_specs(fwd) + [g_spec(fwd)] + qkv_specs(bwd) + [g_spec(bwd)]
        + [pl.BlockSpec((None, 8, LANES), lambda b, g, c: (g, 0, 0)), s_spec, s_spec],
        out_specs=out_specs,
        out_shape=out_shape,
        compiler_params=_params(("parallel", "parallel", "arbitrary")),
        name="delta_rule",
    )(qkv, qkv, qkv, gates, qkv, qkv, qkv, gates, par, s0f, s0b)


def _glu(v_ref, g_ref, rows):
    return v_ref[rows, :].astype(F32) * jax.nn.sigmoid(g_ref[rows, :].astype(F32))


def _conv_h_kernel(v_ref, g_ref, w_ref, o_ref, pad_ref):
    rows, tc = v_ref.shape
    half = CONF_K // 2
    lead = 16
    stride = lead + GRID_W + 16
    total = rows // GRID_W * stride
    w = w_ref[...]
    for g in range(rows // GRID_W):
        base = g * stride
        pad_ref[0, base:base + lead, :] = jnp.zeros((lead, tc), F32)
        pad_ref[0, base + lead + GRID_W:base + stride, :] = jnp.zeros((16, tc), F32)
        pad_ref[0, base + lead:base + lead + GRID_W, :] = _glu(v_ref, g_ref,
                                                               slice(g * GRID_W, (g + 1) * GRID_W))
    for s in range(1, SUBLANES):
        pad_ref[s, 0:total - SUBLANES, :] = pad_ref[0, s:total - SUBLANES + s, :]
    for g in range(rows // GRID_W):
        for cs in range(tc // LANES):
            lanes = slice(cs * LANES, (cs + 1) * LANES)
            acc = None
            for tap in range(CONF_K):
                off = lead - half + tap
                s = off % SUBLANES
                start = g * stride + off - s
                term = w[tap:tap + 1, lanes] * pad_ref[s, start:start + GRID_W, lanes]
                acc = term if acc is None else acc + term
            o_ref[g * GRID_W:(g + 1) * GRID_W, lanes] = acc.astype(o_ref.dtype)


def _conv_h(h_main, w_dw, col_v, col_g, half_c, rows=512, tc=256):
    m = h_main.shape[0]
    ov, og = col_v // tc, col_g // tc
    return pl.pallas_call(
        _conv_h_kernel,
        grid=(m // rows, half_c // tc),
        in_specs=[pl.BlockSpec((rows, tc), lambda i, j: (i, ov + j)),
                  pl.BlockSpec((rows, tc), lambda i, j: (i, og + j)),
                  pl.BlockSpec((CONF_K, tc), lambda i, j: (0, j))],
        out_specs=pl.BlockSpec((rows, tc), lambda i, j: (i, j)),
        out_shape=jax.ShapeDtypeStruct((m, half_c), BF16),
        scratch_shapes=[pltpu.VMEM((SUBLANES, rows // GRID_W * (GRID_W + 32), tc), F32)],
        compiler_params=_params(("parallel", "parallel")),
        name="conv_rows",
    )(h_main, h_main, w_dw)


def _conv_v_kernel(v_ref, g_ref, w_ref, o_ref, pad_ref):
    seq, tc = v_ref.shape
    halo = (CONF_K // 2) * GRID_W
    pad_ref[0:halo, :] = jnp.zeros((halo, tc), F32)
    pad_ref[halo + seq:halo + seq + halo, :] = jnp.zeros((halo, tc), F32)
    blk = 512

    def fill(i, carry):
        r0 = pl.multiple_of(i * blk, blk)
        pad_ref[pl.ds(halo + r0, blk), :] = _glu(v_ref, g_ref, pl.ds(r0, blk))
        return carry

    lax.fori_loop(0, seq // blk, fill, 0)
    w = w_ref[...]

    def body(i, carry):
        r0 = pl.multiple_of(i * GRID_W, GRID_W)
        for cs in range(tc // LANES):
            lanes = slice(cs * LANES, (cs + 1) * LANES)
            acc = None
            for tap in range(CONF_K):
                term = w[tap:tap + 1, lanes] * pad_ref[pl.ds(r0 + tap * GRID_W, GRID_W), lanes]
                acc = term if acc is None else acc + term
            o_ref[pl.ds(r0, GRID_W), lanes] = acc.astype(o_ref.dtype)
        return carry

    lax.fori_loop(0, seq // GRID_W, body, 0)


def _conv_v(h_main, w_dw, col_v, col_g, half_c, seq_len, tc=256):
    m = h_main.shape[0]
    ov, og = col_v // tc, col_g // tc
    ow = half_c // tc
    return pl.pallas_call(
        _conv_v_kernel,
        grid=(m // seq_len, half_c // tc),
        in_specs=[pl.BlockSpec((seq_len, tc), lambda b, j: (b, ov + j)),
                  pl.BlockSpec((seq_len, tc), lambda b, j: (b, og + j)),
                  pl.BlockSpec((CONF_K, tc), lambda b, j: (0, ow + j))],
        out_specs=pl.BlockSpec((seq_len, tc), lambda b, j: (b, j)),
        out_shape=jax.ShapeDtypeStruct((m, half_c), BF16),
        scratch_shapes=[pltpu.VMEM((seq_len + 2 * (CONF_K // 2) * GRID_W, tc), F32)],
        compiler_params=_params(("parallel", "parallel")),
        name="conv_cols",
    )(h_main, h_main, w_dw)


def _mix_kernel(of_ref, ob_ref, z_ref, yh_ref, yv_ref, gn_ref, bdw_ref, lg_ref, lb_ref, o_ref):
    dn = of_ref.shape[1]
    for h in range(dn // LANES):
        lanes = slice(h * LANES, (h + 1) * LANES)
        o = of_ref[:, lanes].astype(F32) + ob_ref[:, lanes].astype(F32)
        ms = jnp.mean(o * o, axis=-1, keepdims=True)
        on = o * lax.rsqrt(ms + 1e-6) * gn_ref[...]
        o_ref[:, lanes] = (on * _silu(z_ref[:, lanes].astype(F32))).astype(o_ref.dtype)
    y = jnp.concatenate([yh_ref[...].astype(F32), yv_ref[...].astype(F32)], axis=1) + bdw_ref[...]
    o_ref[:, dn:] = _silu(_ln(y, lg_ref[...], lb_ref[...])).astype(o_ref.dtype)


def _mix(o_f, o_b, h_main, z_blk, y_h, y_v, gn, bdw, lg, lb, tl=256):
    m, dn = o_f.shape
    half_c = y_h.shape[1]
    conf = 2 * half_c
    row = lambda w: pl.BlockSpec((tl, w), lambda i: (i, 0))
    vec = lambda w: pl.BlockSpec((1, w), lambda i: (0, 0))
    return pl.pallas_call(
        _mix_kernel,
        grid=(m // tl,),
        in_specs=[row(dn), row(dn), pl.BlockSpec((tl, dn), lambda i: (i, z_blk)), row(half_c), row(half_c),
                  vec(LANES), vec(conf), vec(conf), vec(conf)],
        out_specs=row(dn + conf),
        out_shape=jax.ShapeDtypeStruct((m, dn + conf), BF16),
        compiler_params=_params(("parallel",)),
        name="mixer_in",
    )(o_f, o_b, h_main, y_h, y_v, gn.reshape(1, LANES), bdw.reshape(1, conf), lg.reshape(1, conf),
      lb.reshape(1, conf))


def kernel(x, c, ctx, c_ctx, ln_in_g, ln_in_b, w_mod, b_mod, w_in, w_qkv_conv, a_log_f, dt_bias_f,
           a_log_b, dt_bias_b, dn_norm_g, conf_dw_w, conf_dw_b, conf_ln_g, conf_ln_b, w_out, ln1_g, ln1_b,
           w_mlp1, b_mlp1, w_mlp2, b_mlp2, ln2_g, ln2_b):
    assert w_mod.shape[0] == 1, "single-layer trunk only"
    bsz, seq, d = x.shape
    ctx_len = ctx.shape[1]
    nh = a_log_f.shape[1]
    dn = d // 2
    conf = d - dn
    assert dn == nh * LANES and seq % GRID_W == 0 and ctx_len % CHUNK == 0
    z_off, g_off, conf_off = 3 * dn, 4 * dn, 4 * dn + 4 * nh
    alpha = 2.0 ** 0.25
    hb = _tile(nh, 16)
    nhb = nh // hb
    m, mc = bsz * seq, bsz * ctx_len

    wi = w_in[0]
    w_conf = wi[:, conf_off:][None]
    wg = wi[:, g_off:conf_off].reshape(d, 4, nhb, hb).transpose(0, 2, 1, 3).reshape(d, nhb, 4 * hb)
    wg = jnp.pad(wg, ((0, 0), (0, 0), (0, LANES - 4 * hb))).reshape(d, nhb * LANES).astype(BF16)
    zeros = jnp.zeros_like(a_log_f[0])
    par = jnp.stack([jnp.stack([a_log_f[0], zeros, a_log_b[0], zeros]),
                     jnp.stack([dt_bias_f[0], zeros, dt_bias_b[0], zeros])])
    par = par.reshape(2, 4, nhb, hb).transpose(2, 0, 1, 3).reshape(nhb, 2, 4 * hb)
    par = jnp.pad(par, ((0, 0), (0, 6), (0, LANES - 4 * hb)))
    w2 = w_mlp2[0].astype(BF16)

    cc = jnp.concatenate([c, c_ctx[None, :], jnp.zeros((8 - bsz - 1, d), F32)], axis=0)
    mod3 = _mod_table(cc, w_mod[0], b_mod[0]).reshape(8, 1, 6 * d)

    tl = 256
    lat_row = lambda i: (i * tl) // seq
    ctx_row = lambda i: bsz

    uc = _ln_mod(ctx.reshape(mc, d), ln_in_g, ln_in_b, mod3, ctx_row, tl)
    hc = _matmul_wres(uc, w_in, BF16, 3 * dn)
    gates_c = _matmul(uc, wg, F32, tn=nhb * LANES)
    qkv_c = _prep(hc, w_qkv_conv[0], ctx_len, dn)
    s0 = jnp.zeros((bsz, nh // 2, LANES, 2 * LANES), F32)
    s_f, s_b = _delta(qkv_c, gates_c, par, s0, s0, ctx_len, nh, False, hb)

    xr = x.reshape(m, d)
    u0 = _ln_mod(xr, ln_in_g, ln_in_b, mod3, lat_row, tl)
    h_qkvz = _matmul_wres(u0, w_in, BF16, g_off)
    h_conf = _matmul_wres(u0, w_conf, BF16, 2 * conf)
    gates = _matmul(u0, wg, F32, tn=nhb * LANES)
    qkv = _prep(h_qkvz, w_qkv_conv[0], seq, dn)
    o_f, o_b, _, _ = _delta(qkv, gates, par, s_f, s_b, seq, nh, True, hb)
    y_h = _conv_h(h_conf, conf_dw_w[0], 0, conf, conf // 2)
    y_v = _conv_v(h_conf, conf_dw_w[0], conf // 2, conf + conf // 2, conf // 2, seq)
    cat = _mix(o_f, o_b, h_qkvz, z_off // dn, y_h, y_v, dn_norm_g[0], conf_dw_b[0], conf_ln_g[0],
               conf_ln_b[0], tl)
    y = _matmul_wres(cat, w_out, BF16, d)
    u1, x1 = _mlp_in(xr, y, ln_in_g, ln_in_b, ln1_g[0], ln1_b[0], mod3, lat_row, alpha, tl)
    hid = _matmul_wres(u1, w_mlp1, BF16, w_mlp1.shape[2], bias=b_mlp1[0])
    y2 = _matmul_k(hid, w2, BF16)
    out = _final(x1, y2, b_mlp2[0], ln2_g[0], ln2_b[0], mod3, lat_row, alpha, tl)
    return out.reshape(bsz, seq, d)
```

```python
import functools

import jax
import jax.numpy as jnp
from jax import lax
from jax.experimental import pallas as pl
from jax.experimental.pallas import tpu as pltpu

F32 = jnp.float32
BF16 = jnp.bfloat16

GRID_W = 64
CHUNK = 64
INV_BASE = 8
SHORT_CONV = 7
CONF_K = 31
LANES = 128
SUBLANES = 8
BF16_ROWS = 16
VMEM_LIMIT = 56 * 1024 * 1024
LN_EPS = 1e-5


def _params(sem):
    return pltpu.CompilerParams(dimension_semantics=sem, vmem_limit_bytes=VMEM_LIMIT)


def _tile(n, preferred):
    t = min(preferred, n)
    while n % t:
        t //= 2
    return t


def _silu(x):
    return x * jax.nn.sigmoid(x)


def _ln(x, g, b):
    mu = jnp.mean(x, axis=-1, keepdims=True)
    xc = x - mu
    var = jnp.mean(xc * xc, axis=-1, keepdims=True)
    return xc * lax.rsqrt(var + LN_EPS) * g + b


def _mod_kernel(c_ref, w_ref, b_ref, o_ref):
    s = _silu(c_ref[...]).astype(BF16)
    o_ref[...] = jnp.dot(s, w_ref[...].astype(BF16), preferred_element_type=F32) + b_ref[...]


def _mod_table(cc, w, b, tn=512):
    r, d = cc.shape
    n = w.shape[1]
    return pl.pallas_call(
        _mod_kernel,
        grid=(n // tn,),
        in_specs=[pl.BlockSpec((r, d), lambda j: (0, 0)),
                  pl.BlockSpec((d, tn), lambda j: (0, j)),
                  pl.BlockSpec((1, tn), lambda j: (0, j))],
        out_specs=pl.BlockSpec((r, tn), lambda j: (0, j)),
        out_shape=jax.ShapeDtypeStruct((r, n), F32),
        compiler_params=_params(("parallel",)),
        name="mod_table",
    )(cc, w, b.reshape(1, n))


def _mod_spec(d, row_of_tile, chunk):
    return pl.BlockSpec((None, 1, d), lambda i: (row_of_tile(i), 0, chunk))


def _ln_mod_kernel(x_ref, g_ref, b_ref, sh_ref, sc_ref, o_ref):
    xn = _ln(x_ref[...], g_ref[...], b_ref[...])
    o_ref[...] = (xn * (1.0 + sc_ref[...]) + sh_ref[...]).astype(o_ref.dtype)


def _ln_mod(x, g, b, mod3, row_of_tile, tl=256):
    m, d = x.shape
    row = pl.BlockSpec((tl, d), lambda i: (i, 0))
    vec = pl.BlockSpec((1, d), lambda i: (0, 0))
    return pl.pallas_call(
        _ln_mod_kernel,
        grid=(m // tl,),
        in_specs=[row, vec, vec, _mod_spec(d, row_of_tile, 0), _mod_spec(d, row_of_tile, 1)],
        out_specs=row,
        out_shape=jax.ShapeDtypeStruct((m, d), BF16),
        compiler_params=_params(("parallel",)),
        name="ln_mod",
    )(x, g.reshape(1, d), b.reshape(1, d), mod3, mod3)


def _mlp_in_kernel(x_ref, y_ref, gin_ref, bin_ref, ga_ref, g1_ref, b1_ref, sh_ref, sc_ref, u_ref, x1_ref,
                   *, alpha):
    xn = _ln(x_ref[...], gin_ref[...], bin_ref[...])
    x1 = _ln(alpha * xn + ga_ref[...] * y_ref[...].astype(F32), g1_ref[...], b1_ref[...])
    x1_ref[...] = x1
    u_ref[...] = (x1 * (1.0 + sc_ref[...]) + sh_ref[...]).astype(u_ref.dtype)


def _mlp_in(x, y, gin, bin_, g1, b1, mod3, row_of_tile, alpha, tl=256):
    m, d = x.shape
    row = pl.BlockSpec((tl, d), lambda i: (i, 0))
    vec = pl.BlockSpec((1, d), lambda i: (0, 0))
    v = lambda a: a.reshape(1, d)
    return pl.pallas_call(
        functools.partial(_mlp_in_kernel, alpha=alpha),
        grid=(m // tl,),
        in_specs=[row, row, vec, vec, _mod_spec(d, row_of_tile, 2), vec, vec,
                  _mod_spec(d, row_of_tile, 3), _mod_spec(d, row_of_tile, 4)],
        out_specs=[row, row],
        out_shape=[jax.ShapeDtypeStruct((m, d), BF16), jax.ShapeDtypeStruct((m, d), F32)],
        compiler_params=_params(("parallel",)),
        name="mlp_in",
    )(x, y, v(gin), v(bin_), mod3, v(g1), v(b1), mod3, mod3)


def _final_kernel(x1_ref, y2_ref, bm_ref, gm_ref, g2_ref, b2_ref, o_ref, *, alpha):
    y2 = y2_ref[...].astype(F32) + bm_ref[...]
    o_ref[...] = _ln(alpha * x1_ref[...] + gm_ref[...] * y2, g2_ref[...], b2_ref[...])


def _final(x1, y2, bm, g2, b2, mod3, row_of_tile, alpha, tl=256):
    m, d = x1.shape
    row = pl.BlockSpec((tl, d), lambda i: (i, 0))
    vec = pl.BlockSpec((1, d), lambda i: (0, 0))
    v = lambda a: a.reshape(1, d)
    return pl.pallas_call(
        functools.partial(_final_kernel, alpha=alpha),
        grid=(m // tl,),
        in_specs=[row, row, vec, _mod_spec(d, row_of_tile, 5), vec, vec],
        out_specs=row,
        out_shape=jax.ShapeDtypeStruct((m, d), F32),
        compiler_params=_params(("parallel",)),
        name="final_ln",
    )(x1, y2, v(bm), mod3, v(g2), v(b2))


def _mm_kernel(a_ref, w_ref, *rest, relu2):
    if relu2:
        b_ref, o_ref = rest
    else:
        (o_ref,) = rest
    h = jnp.dot(a_ref[...], w_ref[...], preferred_element_type=F32)
    if relu2:
        r = jnp.maximum(h + b_ref[...], 0.0)
        h = r * r
    o_ref[...] = h.astype(o_ref.dtype)


def _matmul(a, w, out_dtype, n_cols=None, bias=None, tm=1024, tn=1024):
    m, k = a.shape
    n = w.shape[1] if n_cols is None else n_cols
    tm, tn = _tile(m, tm), _tile(n, tn)
    specs = [pl.BlockSpec((tm, k), lambda i, j: (i, 0)), pl.BlockSpec((k, tn), lambda i, j: (0, j))]
    args = (a, w)
    if bias is not None:
        specs.append(pl.BlockSpec((1, tn), lambda i, j: (0, j)))
        args += (bias.reshape(1, -1),)
    return pl.pallas_call(
        functools.partial(_mm_kernel, relu2=bias is not None),
        grid=(m // tm, n // tn),
        in_specs=specs,
        out_specs=pl.BlockSpec((tm, tn), lambda i, j: (i, j)),
        out_shape=jax.ShapeDtypeStruct((m, n), out_dtype),
        compiler_params=_params(("parallel", "arbitrary")),
        name="matmul",
    )(*args)


def _mmk_kernel(a_ref, w_ref, o_ref, acc_ref):
    kk = pl.program_id(2)

    @pl.when(kk == 0)
    def _():
        acc_ref[...] = jnp.zeros_like(acc_ref)

    acc_ref[...] += jnp.dot(a_ref[...], w_ref[...], preferred_element_type=F32)

    @pl.when(kk == pl.num_programs(2) - 1)
    def _():
        o_ref[...] = acc_ref[...].astype(o_ref.dtype)


def _matmul_k(a, w, out_dtype, tm=1024, tn=2048, tk=2048):
    m, k = a.shape
    n = w.shape[1]
    tm, tn, tk = _tile(m, tm), _tile(n, tn), _tile(k, tk)
    return pl.pallas_call(
        _mmk_kernel,
        grid=(m // tm, n // tn, k // tk),
        in_specs=[pl.BlockSpec((tm, tk), lambda i, j, kk: (i, kk)),
                  pl.BlockSpec((tk, tn), lambda i, j, kk: (kk, j))],
        out_specs=pl.BlockSpec((tm, tn), lambda i, j, kk: (i, j)),
        out_shape=jax.ShapeDtypeStruct((m, n), out_dtype),
        scratch_shapes=[pltpu.VMEM((tm, tn), F32)],
        compiler_params=_params(("parallel", "parallel", "arbitrary")),
        name="matmul_k",
    )(a, w)


def _shift_matrix(blk, win):
    half = SHORT_CONV // 2
    side = [tap for tap in range(SHORT_CONV) if tap != half]
    r = jnp.arange(blk)[:, None]
    c = jnp.arange(win)[None, :]
    return jnp.concatenate([(c == r + BF16_ROWS - half + tap) for tap in side], axis=0).astype(BF16)


def _prep_kernel(prev_ref, main_ref, next_ref, w_ref, shift_ref, o_ref, *, tiles_per_seq, norm_blocks):
    i, j = pl.program_id(0), pl.program_id(1)
    t = i % tiles_per_seq
    tl, tc = main_ref.shape
    half = SHORT_CONV // 2
    blk = min(LANES, tl)
    win = 2 * blk
    prev = jnp.where(t > 0, prev_ref[...], 0.0)
    nxt = jnp.where(t < tiles_per_seq - 1, next_ref[...], 0.0)
    tail = jnp.zeros((win - blk - 2 * BF16_ROWS, tc), BF16)
    ext = jnp.concatenate([prev, main_ref[...], nxt, tail], axis=0)
    side = [tap for tap in range(SHORT_CONV) if tap != half]
    shift = shift_ref[...]
    w = w_ref[...]
    for b in range(tl // blk):
        rows = slice(b * blk, (b + 1) * blk)
        moved = jnp.dot(shift, ext[b * blk:b * blk + win, :], preferred_element_type=F32)
        acc = w[half:half + 1, :] * main_ref[rows, :].astype(F32)
        for n, tap in enumerate(side):
            acc = acc + w[tap:tap + 1, :] * moved[n * blk:(n + 1) * blk, :]
        y = _silu(acc)
        for h in range(tc // LANES):
            lanes = slice(h * LANES, (h + 1) * LANES)
            yh = y[:, lanes]
            ss = jnp.sum(yh * yh, axis=-1, keepdims=True)
            yn = yh * lax.rsqrt(ss + 1e-6)
            o_ref[rows, lanes] = jnp.where(j < norm_blocks, yn, yh).astype(o_ref.dtype)


def _prep(h_main, w_conv, seq_len, dn, tl=512, tc=512):
    m = h_main.shape[0]
    tl = min(tl, seq_len)
    tiles_per_seq = seq_len // tl
    hb = tl // BF16_ROWS
    last = m // BF16_ROWS - 1
    blk = min(LANES, tl)
    shift = _shift_matrix(blk, 2 * blk)
    return pl.pallas_call(
        functools.partial(_prep_kernel, tiles_per_seq=tiles_per_seq, norm_blocks=2 * dn // tc),
        grid=(m // tl, 3 * dn // tc),
        in_specs=[pl.BlockSpec((BF16_ROWS, tc), lambda i, j: (jnp.maximum(i * hb - 1, 0), j)),
                  pl.BlockSpec((tl, tc), lambda i, j: (i, j)),
                  pl.BlockSpec((BF16_ROWS, tc), lambda i, j: (jnp.minimum((i + 1) * hb, last), j)),
                  pl.BlockSpec((SHORT_CONV, tc), lambda i, j: (0, j)),
                  pl.BlockSpec(shift.shape, lambda i, j: (0, 0))],
        out_specs=pl.BlockSpec((tl, tc), lambda i, j: (i, j)),
        out_shape=jax.ShapeDtypeStruct((m, 3 * dn), BF16),
        compiler_params=_params(("parallel", "parallel")),
        name="prep_qkv",
    )(h_main, h_main, h_main, w_conv, shift)


def _cumsum_rows(x, reverse):
    n = x.shape[0]
    row = lax.broadcasted_iota(jnp.int32, x.shape, 0)
    s = 1
    while s < n:
        if reverse:
            x = x + jnp.where(row < n - s, pltpu.roll(x, n - s, axis=0), 0.0)
        else:
            x = x + jnp.where(row >= s, pltpu.roll(x, s, axis=0), 0.0)
        s *= 2
    return x


def _dot(a, b):
    return jnp.dot(a.astype(BF16), b.astype(BF16), preferred_element_type=F32)


def _delta_units(q_ref, k_ref, v_ref, g_ref, par_ref, s_ref, o_ref, *, rows, hb, reverse, scale):
    c_len = rows.stop - rows.start
    graw = g_ref[rows, :]
    z = graw + par_ref[1:2, :]
    softplus = jnp.maximum(z, 0.0) + jnp.log1p(jnp.exp(-jnp.abs(z)))
    gc = _cumsum_rows(-jnp.exp(par_ref[0:1, :]) * softplus, reverse)
    beta = jax.nn.sigmoid(graw)
    g_last = gc[0:1, :] if reverse else gc[c_len - 1:c_len, :]
    e_in = jnp.exp(gc)
    e_out = jnp.exp(g_last - gc)
    d_last = jnp.exp(g_last)
    gc_t = jnp.transpose(jnp.concatenate([gc, gc], axis=0))
    row = lax.broadcasted_iota(jnp.int32, (c_len, LANES), 0)
    lane = lax.broadcasted_iota(jnp.int32, (c_len, LANES), 1)
    second = lane >= c_len
    col = jnp.where(second, lane - c_len, lane)
    incl = (row <= col) if reverse else (row >= col)
    strict = (row < col) if reverse else (row > col)
    g_col0 = 2 * hb if reverse else 0
    zeros = jnp.zeros((c_len, LANES), BF16)

    def per_head(t, c0, c1, rows):
        return jnp.concatenate([jnp.broadcast_to(t[:, c0:c0 + 1], (rows, LANES)),
                                jnp.broadcast_to(t[:, c1:c1 + 1], (rows, LANES))], axis=1)

    pairs = []
    for p in range(hb // 2):
        cg0, cg1 = g_col0 + 2 * p, g_col0 + 2 * p + 1
        lanes = slice(2 * p * LANES, (2 * p + 2) * LANES)
        k16 = k_ref[rows, lanes]
        q = q_ref[rows, lanes].astype(F32) * scale
        k = k16.astype(F32)
        b = per_head(beta, cg0 + hb, cg1 + hb, c_len)
        ein = per_head(e_in, cg0, cg1, c_len)
        diff = (jnp.where(second, gc[:, cg1:cg1 + 1], gc[:, cg0:cg0 + 1])
                - jnp.where(second[0:1], gc_t[cg1:cg1 + 1, :], gc_t[cg0:cg0 + 1, :]))
        kbeta = k * b
        vb = (v_ref[rows, lanes].astype(F32) * b).astype(BF16)
        kbe = (kbeta * ein).astype(BF16)
        pairs.append(dict(
            pi=p, rows=rows, lanes=lanes, s_ref=s_ref, o_ref=o_ref, strict=strict,
            eye=jnp.where(row == col, 1.0, 0.0),
            gamma=jnp.where(incl, jnp.exp(jnp.where(incl, diff, 0.0)), 0.0),
            lhs1=jnp.concatenate([kbeta, q], axis=0).astype(BF16),
            k_bd=jnp.concatenate([jnp.concatenate([k16[:, :LANES], zeros], axis=1),
                                  jnp.concatenate([zeros, k16[:, LANES:]], axis=1)], axis=0),
            rhs_bd=jnp.concatenate(
                [jnp.concatenate([vb[:, :LANES], zeros, kbe[:, :LANES], zeros], axis=1),
                 jnp.concatenate([zeros, vb[:, LANES:], zeros, kbe[:, LANES:]], axis=1)], axis=0),
            qd=(q * ein).astype(BF16),
            kd=k * per_head(e_out, cg0, cg1, c_len),
            dl=per_head(d_last, cg0, cg1, 1)))
    return pairs


def _same_block_mask(width, c_len):
    shift = c_len.bit_length() - 1
    r = lax.broadcasted_iota(jnp.int32, (width, width), 0)
    c = lax.broadcasted_iota(jnp.int32, (width, width), 1)
    return lax.shift_right_logical(r, shift) == lax.shift_right_logical(c, shift)


def _level_masks(c_len, width):
    r = lax.broadcasted_iota(jnp.int32, (c_len, width), 0)
    c = jnp.bitwise_and(lax.broadcasted_iota(jnp.int32, (c_len, width), 1), c_len - 1)

    def same(s):
        sh = s.bit_length() - 1
        return lax.shift_right_logical(r, sh) == lax.shift_right_logical(c, sh)

    masks = [same(INV_BASE)]
    s = INV_BASE
    while s < c_len:
        masks.append(same(2 * s) & jnp.logical_not(same(s)))
        s *= 2
    return masks


def _block_diag(x16, same):
    n = x16.shape[1] // x16.shape[0]
    return jnp.where(same, jnp.concatenate([x16] * n, axis=0), 0.0)


def _delta_kernel(*refs, hb, scale, write_o):
    (qf, kf, vf, gf, qb, kb, vb, gb, par, s0f, s0b), outs = refs[:11], refs[11:]
    if write_o:
        of, ob, sf, sb = outs
    else:
        (sf, sb), of, ob = outs, None, None

    @pl.when(pl.program_id(2) == 0)
    def _():
        sf[...] = s0f[...]
        sb[...] = s0b[...]

    c_len = CHUNK
    n_sub = qf.shape[0] // c_len
    phases = []
    for j in range(n_sub):
        rf = slice(j * c_len, (j + 1) * c_len)
        rb = slice((n_sub - 1 - j) * c_len, (n_sub - j) * c_len)
        phases.append(_delta_units(qf, kf, vf, gf, par, sf, of, rows=rf, hb=hb, reverse=False, scale=scale)
                      + _delta_units(qb, kb, vb, gb, par, sb, ob, rows=rb, hb=hb, reverse=True, scale=scale))
    units = [u for ph in phases for u in ph]
    dn_t = (((1,), (1,)), ((), ()))

    for u in units:
        m1 = lax.dot_general(u["lhs1"], u["k_bd"], dn_t, preferred_element_type=F32)
        u["a"] = jnp.where(u["strict"], m1[:c_len] * u["gamma"], 0.0)
        u["attn"] = (m1[c_len:] * u["gamma"]).astype(BF16)
    quads = [dict(pairs=units[i:i + 2]) for i in range(0, len(units), 2)]
    bd_masks, lvl_masks = {}, {}
    for qd in quads:
        a = jnp.concatenate([u["a"] for u in qd["pairs"]], axis=1)
        width = a.shape[1]
        if width not in bd_masks:
            bd_masks[width] = _same_block_mask(width, c_len)
            lvl_masks[width] = _level_masks(c_len, width)
        qd["same"] = bd_masks[width]
        same_base, qd["levels"] = lvl_masks[width][0], lvl_masks[width][1:]
        qd["a"] = a
        d = jnp.where(same_base, a, 0.0)
        qd["t"] = jnp.concatenate([u["eye"] for u in qd["pairs"]], axis=1) - d
        d16 = d.astype(BF16)
        qd["dpow"] = jnp.dot(d16, _block_diag(d16, qd["same"]), preferred_element_type=F32).astype(BF16)
    rounds = INV_BASE.bit_length() - 2
    for i in range(rounds):
        for qd in quads:
            bd = _block_diag(qd["dpow"], qd["same"])
            if i + 1 < rounds:
                res = jnp.dot(jnp.concatenate([qd["t"].astype(BF16), qd["dpow"]], axis=0), bd,
                              preferred_element_type=F32)
                qd["t"] = qd["t"] + res[:c_len]
                qd["dpow"] = res[c_len:].astype(BF16)
            else:
                qd["t"] = qd["t"] + jnp.dot(qd["t"].astype(BF16), bd, preferred_element_type=F32)
    for lvl in range(len(quads[0]["levels"])):
        for qd in quads:
            a_s = jnp.where(qd["levels"][lvl], qd["a"], 0.0).astype(BF16)
            qd["n"] = jnp.dot(a_s, _block_diag(qd["t"].astype(BF16), qd["same"]),
                              preferred_element_type=F32).astype(BF16)
        for qd in quads:
            qd["t"] = qd["t"] - jnp.dot(qd["t"].astype(BF16), _block_diag(qd["n"], qd["same"]),
                                        preferred_element_type=F32)
    for qd in quads:
        for j, u in enumerate(qd["pairs"]):
            t16 = qd["t"][:, 2 * j * c_len:2 * (j + 1) * c_len].astype(BF16)
            u["uw"] = jnp.dot(t16, u["rhs_bd"], preferred_element_type=F32)
    zeros = jnp.zeros((LANES, LANES), BF16)
    zc = jnp.zeros((c_len, LANES), BF16)
    for u in units:
        u["kd_t"] = jnp.transpose(
            jnp.concatenate([u["kd"][:, :LANES], u["kd"][:, LANES:]], axis=0)).astype(BF16)
    for phase in phases:
        for u in phase:
            s = u["s_ref"][u["pi"]]
            s16 = s.astype(BF16)
            s_bd = jnp.concatenate([jnp.concatenate([s16[:, :LANES], zeros], axis=1),
                                    jnp.concatenate([zeros, s16[:, LANES:]], axis=1)], axis=0)
            w = u["uw"][:, 2 * LANES:].astype(BF16)
            ws = jnp.dot(jnp.concatenate([w, u["qd"]], axis=0), s_bd, preferred_element_type=F32)
            vn = (u["uw"][:, :2 * LANES] - ws[:c_len]).astype(BF16)
            u["vn_bd"] = jnp.concatenate([jnp.concatenate([vn[:, :LANES], zc], axis=1),
                                          jnp.concatenate([zc, vn[:, LANES:]], axis=1)], axis=0)
            u["o"] = ws[c_len:]
            u["s"] = s
        for u in phase:
            if u["o_ref"] is None:
                ds = jnp.dot(u["kd_t"], u["vn_bd"], preferred_element_type=F32)
            else:
                res = jnp.dot(jnp.concatenate([u["attn"], u["kd_t"]], axis=0), u["vn_bd"],
                              preferred_element_type=F32)
                u["o_ref"][u["rows"], u["lanes"]] = (u["o"] + res[:c_len]).astype(u["o_ref"].dtype)
                ds = res[c_len:]
            u["s_ref"][u["pi"]] = u["s"] * u["dl"] + ds


def _delta(qkv, gates, par, s0f, s0b, seq_len, nh, write_o, hb=4, rows=2 * CHUNK):
    m = qkv.shape[0]
    bsz = m // seq_len
    assert hb % 4 == 0 and seq_len % rows == 0 and rows % CHUNK == 0
    nc = seq_len // rows
    nhb = nh // hb
    w = hb * LANES
    fwd = lambda b, g, c: b * nc + c
    bwd = lambda b, g, c: b * nc + nc - 1 - c

    def qkv_specs(row):
        return [pl.BlockSpec((rows, w), lambda b, g, c, o=o: (row(b, g, c), o * nhb + g)) for o in range(3)]

    g_spec = lambda row: pl.BlockSpec((rows, LANES), lambda b, g, c: (row(b, g, c), g))
    s_spec = pl.BlockSpec((None, hb // 2, LANES, 2 * LANES), lambda b, g, c: (b, g, 0, 0))
    o_spec = lambda row: pl.BlockSpec((rows, w), lambda b, g, c: (row(b, g, c), g))
    s_shape = jax.ShapeDtypeStruct((bsz, nh // 2, LANES, 2 * LANES), F32)
    o_shape = jax.ShapeDtypeStruct((m, nh * LANES), BF16)
    out_specs = [s_spec, s_spec]
    out_shape = [s_shape, s_shape]
    if write_o:
        out_specs = [o_spec(fwd), o_spec(bwd)] + out_specs
        out_shape = [o_shape, o_shape] + out_shape
    return pl.pallas_call(
        functools.partial(_delta_kernel, hb=hb, scale=float(LANES) ** -0.5, write_o=write_o),
        grid=(bsz, nhb, nc),
        in_specs=qkv_specs(fwd) + [g_spec(fwd)] + qkv_specs(bwd) + [g_spec(bwd)]
        + [pl.BlockSpec((None, 8, LANES), lambda b, g, c: (g, 0, 0)), s_spec, s_spec],
        out_specs=out_specs,
        out_shape=out_shape,
        compiler_params=_params(("parallel", "parallel", "arbitrary")),
        name="delta_rule",
    )(qkv, qkv, qkv, gates, qkv, qkv, qkv, gates, par, s0f, s0b)


def _glu(v_ref, g_ref, rows):
    return v_ref[rows, :].astype(F32) * jax.nn.sigmoid(g_ref[rows, :].astype(F32))


def _conv_h_kernel(v_ref, g_ref, w_ref, o_ref, pad_ref):
    rows, tc = v_ref.shape
    half = CONF_K // 2
    lead = 16
    stride = lead + GRID_W + 16
    total = rows // GRID_W * stride
    w = w_ref[...]
    for g in range(rows // GRID_W):
        base = g * stride
        pad_ref[0, base:base + lead, :] = jnp.zeros((lead, tc), F32)
        pad_ref[0, base + lead + GRID_W:base + stride, :] = jnp.zeros((16, tc), F32)
        pad_ref[0, base + lead:base + lead + GRID_W, :] = _glu(v_ref, g_ref,
                                                               slice(g * GRID_W, (g + 1) * GRID_W))
    for s in range(1, SUBLANES):
        pad_ref[s, 0:total - SUBLANES, :] = pad_ref[0, s:total - SUBLANES + s, :]
    for g in range(rows // GRID_W):
        for cs in range(tc // LANES):
            lanes = slice(cs * LANES, (cs + 1) * LANES)
            acc = None
            for tap in range(CONF_K):
                off = lead - half + tap
                s = off % SUBLANES
                start = g * stride + off - s
                term = w[tap:tap + 1, lanes] * pad_ref[s, start:start + GRID_W, lanes]
                acc = term if acc is None else acc + term
            o_ref[g * GRID_W:(g + 1) * GRID_W, lanes] = acc.astype(o_ref.dtype)


def _conv_h(h_main, w_dw, col_v, col_g, half_c, rows=512, tc=256):
    m = h_main.shape[0]
    ov, og = col_v // tc, col_g // tc
    return pl.pallas_call(
        _conv_h_kernel,
        grid=(m // rows, half_c // tc),
        in_specs=[pl.BlockSpec((rows, tc), lambda i, j: (i, ov + j)),
                  pl.BlockSpec((rows, tc), lambda i, j: (i, og + j)),
                  pl.BlockSpec((CONF_K, tc), lambda i, j: (0, j))],
        out_specs=pl.BlockSpec((rows, tc), lambda i, j: (i, j)),
        out_shape=jax.ShapeDtypeStruct((m, half_c), BF16),
        scratch_shapes=[pltpu.VMEM((SUBLANES, rows // GRID_W * (GRID_W + 32), tc), F32)],
        compiler_params=_params(("parallel", "parallel")),
        name="conv_rows",
    )(h_main, h_main, w_dw)


def _conv_v_kernel(v_ref, g_ref, w_ref, o_ref, pad_ref):
    seq, tc = v_ref.shape
    halo = (CONF_K // 2) * GRID_W
    pad_ref[0:halo, :] = jnp.zeros((halo, tc), F32)
    pad_ref[halo + seq:halo + seq + halo, :] = jnp.zeros((halo, tc), F32)
    blk = 512

    def fill(i, carry):
        r0 = pl.multiple_of(i * blk, blk)
        pad_ref[pl.ds(halo + r0, blk), :] = _glu(v_ref, g_ref, pl.ds(r0, blk))
        return carry

    lax.fori_loop(0, seq // blk, fill, 0)
    w = w_ref[...]

    def body(i, carry):
        r0 = pl.multiple_of(i * GRID_W, GRID_W)
        for cs in range(tc // LANES):
            lanes = slice(cs * LANES, (cs + 1) * LANES)
            acc = None
            for tap in range(CONF_K):
                term = w[tap:tap + 1, lanes] * pad_ref[pl.ds(r0 + tap * GRID_W, GRID_W), lanes]
                acc = term if acc is None else acc + term
            o_ref[pl.ds(r0, GRID_W), lanes] = acc.astype(o_ref.dtype)
        return carry

    lax.fori_loop(0, seq // GRID_W, body, 0)


def _conv_v(h_main, w_dw, col_v, col_g, half_c, seq_len, tc=256):
    m = h_main.shape[0]
    ov, og = col_v // tc, col_g // tc
    ow = half_c // tc
    return pl.pallas_call(
        _conv_v_kernel,
        grid=(m // seq_len, half_c // tc),
        in_specs=[pl.BlockSpec((seq_len, tc), lambda b, j: (b, ov + j)),
                  pl.BlockSpec((seq_len, tc), lambda b, j: (b, og + j)),
                  pl.BlockSpec((CONF_K, tc), lambda b, j: (0, ow + j))],
        out_specs=pl.BlockSpec((seq_len, tc), lambda b, j: (b, j)),
        out_shape=jax.ShapeDtypeStruct((m, half_c), BF16),
        scratch_shapes=[pltpu.VMEM((seq_len + 2 * (CONF_K // 2) * GRID_W, tc), F32)],
        compiler_params=_params(("parallel", "parallel")),
        name="conv_cols",
    )(h_main, h_main, w_dw)


def _mix_kernel(of_ref, ob_ref, z_ref, yh_ref, yv_ref, gn_ref, bdw_ref, lg_ref, lb_ref, o_ref):
    dn = of_ref.shape[1]
    for h in range(dn // LANES):
        lanes = slice(h * LANES, (h + 1) * LANES)
        o = of_ref[:, lanes].astype(F32) + ob_ref[:, lanes].astype(F32)
        ms = jnp.mean(o * o, axis=-1, keepdims=True)
        on = o * lax.rsqrt(ms + 1e-6) * gn_ref[...]
        o_ref[:, lanes] = (on * _silu(z_ref[:, lanes].astype(F32))).astype(o_ref.dtype)
    y = jnp.concatenate([yh_ref[...].astype(F32), yv_ref[...].astype(F32)], axis=1) + bdw_ref[...]
    o_ref[:, dn:] = _silu(_ln(y, lg_ref[...], lb_ref[...])).astype(o_ref.dtype)


def _mix(o_f, o_b, h_main, z_blk, y_h, y_v, gn, bdw, lg, lb, tl=256):
    m, dn = o_f.shape
    half_c = y_h.shape[1]
    conf = 2 * half_c
    row = lambda w: pl.BlockSpec((tl, w), lambda i: (i, 0))
    vec = lambda w: pl.BlockSpec((1, w), lambda i: (0, 0))
    return pl.pallas_call(
        _mix_kernel,
        grid=(m // tl,),
        in_specs=[row(dn), row(dn), pl.BlockSpec((tl, dn), lambda i: (i, z_blk)), row(half_c), row(half_c),
                  vec(LANES), vec(conf), vec(conf), vec(conf)],
        out_specs=row(dn + conf),
        out_shape=jax.ShapeDtypeStruct((m, dn + conf), BF16),
        compiler_params=_params(("parallel",)),
        name="mixer_in",
    )(o_f, o_b, h_main, y_h, y_v, gn.reshape(1, LANES), bdw.reshape(1, conf), lg.reshape(1, conf),
      lb.reshape(1, conf))


def kernel(x, c, ctx, c_ctx, ln_in_g, ln_in_b, w_mod, b_mod, w_in, w_qkv_conv, a_log_f, dt_bias_f,
           a_log_b, dt_bias_b, dn_norm_g, conf_dw_w, conf_dw_b, conf_ln_g, conf_ln_b, w_out, ln1_g, ln1_b,
           w_mlp1, b_mlp1, w_mlp2, b_mlp2, ln2_g, ln2_b):
    assert w_mod.shape[0] == 1, "single-layer trunk only"
    bsz, seq, d = x.shape
    ctx_len = ctx.shape[1]
    nh = a_log_f.shape[1]
    dn = d // 2
    conf = d - dn
    assert dn == nh * LANES and seq % GRID_W == 0 and ctx_len % CHUNK == 0
    z_off, g_off, conf_off = 3 * dn, 4 * dn, 4 * dn + 4 * nh
    alpha = 2.0 ** 0.25
    hb = _tile(nh, 16)
    nhb = nh // hb
    m, mc = bsz * seq, bsz * ctx_len

    wi = w_in[0]
    w_qkvz = wi[:, :g_off].astype(BF16)
    w_conf = wi[:, conf_off:].astype(BF16)
    wg = wi[:, g_off:conf_off].reshape(d, 4, nhb, hb).transpose(0, 2, 1, 3).reshape(d, nhb, 4 * hb)
    wg = jnp.pad(wg, ((0, 0), (0, 0), (0, LANES - 4 * hb))).reshape(d, nhb * LANES).astype(BF16)
    zeros = jnp.zeros_like(a_log_f[0])
    par = jnp.stack([jnp.stack([a_log_f[0], zeros, a_log_b[0], zeros]),
                     jnp.stack([dt_bias_f[0], zeros, dt_bias_b[0], zeros])])
    par = par.reshape(2, 4, nhb, hb).transpose(2, 0, 1, 3).reshape(nhb, 2, 4 * hb)
    par = jnp.pad(par, ((0, 0), (0, 6), (0, LANES - 4 * hb)))
    w_o = w_out[0].astype(BF16)
    w1 = w_mlp1[0].astype(BF16)
    w2 = w_mlp2[0].astype(BF16)

    cc = jnp.concatenate([c, c_ctx[None, :], jnp.zeros((8 - bsz - 1, d), F32)], axis=0)
    mod3 = _mod_table(cc, w_mod[0], b_mod[0]).reshape(8, 1, 6 * d)

    tl = 256
    lat_row = lambda i: (i * tl) // seq
    ctx_row = lambda i: bsz

    uc = _ln_mod(ctx.reshape(mc, d), ln_in_g, ln_in_b, mod3, ctx_row, tl)
    hc = _matmul(uc, w_qkvz, BF16, n_cols=3 * dn)
    gates_c = _matmul(uc, wg, F32, tn=nhb * LANES)
    qkv_c = _prep(hc, w_qkv_conv[0], ctx_len, dn)
    s0 = jnp.zeros((bsz, nh // 2, LANES, 2 * LANES), F32)
    s_f, s_b = _delta(qkv_c, gates_c, par, s0, s0, ctx_len, nh, False, hb)

    xr = x.reshape(m, d)
    u0 = _ln_mod(xr, ln_in_g, ln_in_b, mod3, lat_row, tl)
    h_qkvz = _matmul(u0, w_qkvz, BF16)
    h_conf = _matmul(u0, w_conf, BF16)
    gates = _matmul(u0, wg, F32, tn=nhb * LANES)
    qkv = _prep(h_qkvz, w_qkv_conv[0], seq, dn)
    o_f, o_b, _, _ = _delta(qkv, gates, par, s_f, s_b, seq, nh, True, hb)
    y_h = _conv_h(h_conf, conf_dw_w[0], 0, conf, conf // 2)
    y_v = _conv_v(h_conf, conf_dw_w[0], conf // 2, conf + conf // 2, conf // 2, seq)
    cat = _mix(o_f, o_b, h_qkvz, z_off // dn, y_h, y_v, dn_norm_g[0], conf_dw_b[0], conf_ln_g[0],
               conf_ln_b[0], tl)
    y = _matmul(cat, w_o, BF16)
    u1, x1 = _mlp_in(xr, y, ln_in_g, ln_in_b, ln1_g[0], ln1_b[0], mod3, lat_row, alpha, tl)
    hid = _matmul(u1, w1, BF16, bias=b_mlp1[0])
    y2 = _matmul_k(hid, w2, BF16)
    out = _final(x1, y2, b_mlp2[0], ln2_g[0], ln2_b[0], mod3, lat_row, alpha, tl)
    return out.reshape(bsz, seq, d)
```

```python
import functools

import jax
import jax.numpy as jnp
from jax import lax
from jax.experimental import pallas as pl
from jax.experimental.pallas import tpu as pltpu

F32 = jnp.float32
BF16 = jnp.bfloat16

GRID_W = 64
CHUNK = 64
INV_BASE = 8
SHORT_CONV = 7
CONF_K = 31
LANES = 128
SUBLANES = 8
BF16_ROWS = 16
VMEM_LIMIT = 56 * 1024 * 1024
LN_EPS = 1e-5


def _params(sem):
    return pltpu.CompilerParams(dimension_semantics=sem, vmem_limit_bytes=VMEM_LIMIT)


def _tile(n, preferred):
    t = min(preferred, n)
    while n % t:
        t //= 2
    return t


def _silu(x):
    return x * jax.nn.sigmoid(x)


def _ln(x, g, b):
    mu = jnp.mean(x, axis=-1, keepdims=True)
    xc = x - mu
    var = jnp.mean(xc * xc, axis=-1, keepdims=True)
    return xc * lax.rsqrt(var + LN_EPS) * g + b


def _mod_kernel(c_ref, w_ref, b_ref, o_ref):
    s = _silu(c_ref[...]).astype(BF16)
    o_ref[...] = jnp.dot(s, w_ref[...].astype(BF16), preferred_element_type=F32) + b_ref[...]


def _mod_table(cc, w, b, tn=512):
    r, d = cc.shape
    n = w.shape[1]
    return pl.pallas_call(
        _mod_kernel,
        grid=(n // tn,),
        in_specs=[pl.BlockSpec((r, d), lambda j: (0, 0)),
                  pl.BlockSpec((d, tn), lambda j: (0, j)),
                  pl.BlockSpec((1, tn), lambda j: (0, j))],
        out_specs=pl.BlockSpec((r, tn), lambda j: (0, j)),
        out_shape=jax.ShapeDtypeStruct((r, n), F32),
        compiler_params=_params(("parallel",)),
        name="mod_table",
    )(cc, w, b.reshape(1, n))


def _mod_spec(d, row_of_tile, chunk):
    return pl.BlockSpec((None, 1, d), lambda i: (row_of_tile(i), 0, chunk))


def _ln_mod_kernel(x_ref, g_ref, b_ref, sh_ref, sc_ref, o_ref):
    xn = _ln(x_ref[...], g_ref[...], b_ref[...])
    o_ref[...] = (xn * (1.0 + sc_ref[...]) + sh_ref[...]).astype(o_ref.dtype)


def _ln_mod(x, g, b, mod3, row_of_tile, tl=256):
    m, d = x.shape
    row = pl.BlockSpec((tl, d), lambda i: (i, 0))
    vec = pl.BlockSpec((1, d), lambda i: (0, 0))
    return pl.pallas_call(
        _ln_mod_kernel,
        grid=(m // tl,),
        in_specs=[row, vec, vec, _mod_spec(d, row_of_tile, 0), _mod_spec(d, row_of_tile, 1)],
        out_specs=row,
        out_shape=jax.ShapeDtypeStruct((m, d), BF16),
        compiler_params=_params(("parallel",)),
        name="ln_mod",
    )(x, g.reshape(1, d), b.reshape(1, d), mod3, mod3)


def _mlp_in_kernel(x_ref, y_ref, gin_ref, bin_ref, ga_ref, g1_ref, b1_ref, sh_ref, sc_ref, u_ref, x1_ref,
                   *, alpha):
    xn = _ln(x_ref[...], gin_ref[...], bin_ref[...])
    x1 = _ln(alpha * xn + ga_ref[...] * y_ref[...].astype(F32), g1_ref[...], b1_ref[...])
    x1_ref[...] = x1
    u_ref[...] = (x1 * (1.0 + sc_ref[...]) + sh_ref[...]).astype(u_ref.dtype)


def _mlp_in(x, y, gin, bin_, g1, b1, mod3, row_of_tile, alpha, tl=256):
    m, d = x.shape
    row = pl.BlockSpec((tl, d), lambda i: (i, 0))
    vec = pl.BlockSpec((1, d), lambda i: (0, 0))
    v = lambda a: a.reshape(1, d)
    return pl.pallas_call(
        functools.partial(_mlp_in_kernel, alpha=alpha),
        grid=(m // tl,),
        in_specs=[row, row, vec, vec, _mod_spec(d, row_of_tile, 2), vec, vec,
                  _mod_spec(d, row_of_tile, 3), _mod_spec(d, row_of_tile, 4)],
        out_specs=[row, row],
        out_shape=[jax.ShapeDtypeStruct((m, d), BF16), jax.ShapeDtypeStruct((m, d), F32)],
        compiler_params=_params(("parallel",)),
        name="mlp_in",
    )(x, y, v(gin), v(bin_), mod3, v(g1), v(b1), mod3, mod3)


def _final_kernel(x1_ref, y2_ref, bm_ref, gm_ref, g2_ref, b2_ref, o_ref, *, alpha):
    y2 = y2_ref[...].astype(F32) + bm_ref[...]
    o_ref[...] = _ln(alpha * x1_ref[...] + gm_ref[...] * y2, g2_ref[...], b2_ref[...])


def _final(x1, y2, bm, g2, b2, mod3, row_of_tile, alpha, tl=256):
    m, d = x1.shape
    row = pl.BlockSpec((tl, d), lambda i: (i, 0))
    vec = pl.BlockSpec((1, d), lambda i: (0, 0))
    v = lambda a: a.reshape(1, d)
    return pl.pallas_call(
        functools.partial(_final_kernel, alpha=alpha),
        grid=(m // tl,),
        in_specs=[row, row, vec, _mod_spec(d, row_of_tile, 5), vec, vec],
        out_specs=row,
        out_shape=jax.ShapeDtypeStruct((m, d), F32),
        compiler_params=_params(("parallel",)),
        name="final_ln",
    )(x1, y2, v(bm), mod3, v(g2), v(b2))


def _cast_cols_kernel(*refs, off):
    if off == 0:
        lo_ref, o_ref = refs
        o_ref[...] = lo_ref[...].astype(o_ref.dtype)
    else:
        lo_ref, hi_ref, o_ref = refs
        tn = o_ref.shape[1]
        o_ref[:, :tn - off] = lo_ref[:, off:].astype(o_ref.dtype)
        o_ref[:, tn - off:] = hi_ref[:, :off].astype(o_ref.dtype)


def _cast_cols(w3, col0, n_cols, tr=512, tn=1024):
    k, n_total = w3.shape[1], w3.shape[2]
    tr, tn = _tile(k, tr), _tile(n_cols, tn)
    b0, off = divmod(col0, tn)
    assert col0 + n_cols <= n_total
    specs = [pl.BlockSpec((None, tr, tn), lambda i, j: (0, i, b0 + j))]
    args = (w3,)
    if off:
        specs.append(pl.BlockSpec((None, tr, tn), lambda i, j: (0, i, b0 + j + 1)))
        args += (w3,)
    return pl.pallas_call(
        functools.partial(_cast_cols_kernel, off=off),
        grid=(k // tr, n_cols // tn),
        in_specs=specs,
        out_specs=pl.BlockSpec((tr, tn), lambda i, j: (i, j)),
        out_shape=jax.ShapeDtypeStruct((k, n_cols), BF16),
        compiler_params=_params(("parallel", "parallel")),
        name="cast_cols",
    )(*args)


def _mm_kernel(a_ref, w_ref, *rest, relu2):
    if relu2:
        b_ref, o_ref = rest
    else:
        (o_ref,) = rest
    h = jnp.dot(a_ref[...], w_ref[...], preferred_element_type=F32)
    if relu2:
        r = jnp.maximum(h + b_ref[...], 0.0)
        h = r * r
    o_ref[...] = h.astype(o_ref.dtype)


def _matmul(a, w, out_dtype, n_cols=None, bias=None, tm=1024, tn=1024):
    m, k = a.shape
    n = w.shape[1] if n_cols is None else n_cols
    tm, tn = _tile(m, tm), _tile(n, tn)
    specs = [pl.BlockSpec((tm, k), lambda i, j: (i, 0)), pl.BlockSpec((k, tn), lambda i, j: (0, j))]
    args = (a, w)
    if bias is not None:
        specs.append(pl.BlockSpec((1, tn), lambda i, j: (0, j)))
        args += (bias.reshape(1, -1),)
    return pl.pallas_call(
        functools.partial(_mm_kernel, relu2=bias is not None),
        grid=(m // tm, n // tn),
        in_specs=specs,
        out_specs=pl.BlockSpec((tm, tn), lambda i, j: (i, j)),
        out_shape=jax.ShapeDtypeStruct((m, n), out_dtype),
        compiler_params=_params(("parallel", "arbitrary")),
        name="matmul",
    )(*args)


def _mmk_kernel(a_ref, w_ref, o_ref, acc_ref):
    kk = pl.program_id(2)

    @pl.when(kk == 0)
    def _():
        acc_ref[...] = jnp.zeros_like(acc_ref)

    acc_ref[...] += jnp.dot(a_ref[...], w_ref[...], preferred_element_type=F32)

    @pl.when(kk == pl.num_programs(2) - 1)
    def _():
        o_ref[...] = acc_ref[...].astype(o_ref.dtype)


def _matmul_k(a, w, out_dtype, tm=1024, tn=2048, tk=2048):
    m, k = a.shape
    n = w.shape[1]
    tm, tn, tk = _tile(m, tm), _tile(n, tn), _tile(k, tk)
    return pl.pallas_call(
        _mmk_kernel,
        grid=(m // tm, n // tn, k // tk),
        in_specs=[pl.BlockSpec((tm, tk), lambda i, j, kk: (i, kk)),
                  pl.BlockSpec((tk, tn), lambda i, j, kk: (kk, j))],
        out_specs=pl.BlockSpec((tm, tn), lambda i, j, kk: (i, j)),
        out_shape=jax.ShapeDtypeStruct((m, n), out_dtype),
        scratch_shapes=[pltpu.VMEM((tm, tn), F32)],
        compiler_params=_params(("parallel", "parallel", "arbitrary")),
        name="matmul_k",
    )(a, w)


def _shift_matrix(blk, win):
    half = SHORT_CONV // 2
    cols = -(-SHORT_CONV * win // LANES) * LANES
    r = jnp.arange(blk)[:, None]
    c = jnp.arange(cols)[None, :]
    hit = c < 0
    for tap in range(SHORT_CONV):
        hit = hit | (c == tap * win + r + BF16_ROWS - half + tap)
    return hit.astype(BF16)


def _prep_kernel(prev_ref, main_ref, next_ref, w_ref, shift_ref, o_ref, *, tiles_per_seq, norm_blocks):
    i, j = pl.program_id(0), pl.program_id(1)
    t = i % tiles_per_seq
    tl, tc = main_ref.shape
    blk = min(LANES, tl)
    win = blk + 2 * BF16_ROWS
    prev = jnp.where(t > 0, prev_ref[...], 0.0)
    nxt = jnp.where(t < tiles_per_seq - 1, next_ref[...], 0.0)
    ext = jnp.concatenate([prev, main_ref[...], nxt], axis=0)
    w16 = w_ref[...].astype(BF16)
    scaled = [ext * w16[tap:tap + 1, :] for tap in range(SHORT_CONV)]
    shift = shift_ref[...]
    pad = jnp.zeros((shift.shape[1] - SHORT_CONV * win, tc), BF16)
    for b in range(tl // blk):
        rows = slice(b * blk, (b + 1) * blk)
        stacked = jnp.concatenate([z[b * blk:b * blk + win, :] for z in scaled] + [pad], axis=0)
        y = _silu(jnp.dot(shift, stacked, preferred_element_type=F32))
        for h in range(tc // LANES):
            lanes = slice(h * LANES, (h + 1) * LANES)
            yh = y[:, lanes]
            ss = jnp.sum(yh * yh, axis=-1, keepdims=True)
            yn = yh * lax.rsqrt(ss + 1e-6)
            o_ref[rows, lanes] = jnp.where(j < norm_blocks, yn, yh).astype(o_ref.dtype)


def _prep(h_main, w_conv, seq_len, dn, tl=512, tc=512):
    m = h_main.shape[0]
    tl = min(tl, seq_len)
    tiles_per_seq = seq_len // tl
    hb = tl // BF16_ROWS
    last = m // BF16_ROWS - 1
    blk = min(LANES, tl)
    shift = _shift_matrix(blk, blk + 2 * BF16_ROWS)
    return pl.pallas_call(
        functools.partial(_prep_kernel, tiles_per_seq=tiles_per_seq, norm_blocks=2 * dn // tc),
        grid=(m // tl, 3 * dn // tc),
        in_specs=[pl.BlockSpec((BF16_ROWS, tc), lambda i, j: (jnp.maximum(i * hb - 1, 0), j)),
                  pl.BlockSpec((tl, tc), lambda i, j: (i, j)),
                  pl.BlockSpec((BF16_ROWS, tc), lambda i, j: (jnp.minimum((i + 1) * hb, last), j)),
                  pl.BlockSpec((SHORT_CONV, tc), lambda i, j: (0, j)),
                  pl.BlockSpec(shift.shape, lambda i, j: (0, 0))],
        out_specs=pl.BlockSpec((tl, tc), lambda i, j: (i, j)),
        out_shape=jax.ShapeDtypeStruct((m, 3 * dn), BF16),
        compiler_params=_params(("parallel", "parallel")),
        name="prep_qkv",
    )(h_main, h_main, h_main, w_conv, shift)


def _cumsum_rows(x, reverse):
    n = x.shape[0]
    row = lax.broadcasted_iota(jnp.int32, x.shape, 0)
    s = 1
    while s < n:
        if reverse:
            x = x + jnp.where(row < n - s, pltpu.roll(x, n - s, axis=0), 0.0)
        else:
            x = x + jnp.where(row >= s, pltpu.roll(x, s, axis=0), 0.0)
        s *= 2
    return x


def _dot(a, b):
    return jnp.dot(a.astype(BF16), b.astype(BF16), preferred_element_type=F32)


def _delta_units(q_ref, k_ref, v_ref, g_ref, par_ref, s_ref, o_ref, *, rows, hb, reverse, scale):
    c_len = rows.stop - rows.start
    graw = g_ref[rows, :]
    z = graw + par_ref[1:2, :]
    softplus = jnp.maximum(z, 0.0) + jnp.log1p(jnp.exp(-jnp.abs(z)))
    gc = _cumsum_rows(-jnp.exp(par_ref[0:1, :]) * softplus, reverse)
    beta = jax.nn.sigmoid(graw)
    g_last = gc[0:1, :] if reverse else gc[c_len - 1:c_len, :]
    e_in = jnp.exp(gc)
    e_out = jnp.exp(g_last - gc)
    d_last = jnp.exp(g_last)
    gc_t = jnp.transpose(jnp.concatenate([gc, gc], axis=0))
    row = lax.broadcasted_iota(jnp.int32, (c_len, LANES), 0)
    lane = lax.broadcasted_iota(jnp.int32, (c_len, LANES), 1)
    second = lane >= c_len
    col = jnp.where(second, lane - c_len, lane)
    incl = (row <= col) if reverse else (row >= col)
    strict = (row < col) if reverse else (row > col)
    g_col0 = 2 * hb if reverse else 0
    zeros = jnp.zeros((c_len, LANES), BF16)

    def per_head(t, c0, c1, rows):
        return jnp.concatenate([jnp.broadcast_to(t[:, c0:c0 + 1], (rows, LANES)),
                                jnp.broadcast_to(t[:, c1:c1 + 1], (rows, LANES))], axis=1)

    pairs = []
    for p in range(hb // 2):
        cg0, cg1 = g_col0 + 2 * p, g_col0 + 2 * p + 1
        lanes = slice(2 * p * LANES, (2 * p + 2) * LANES)
        k16 = k_ref[rows, lanes]
        q = q_ref[rows, lanes].astype(F32) * scale
        k = k16.astype(F32)
        b = per_head(beta, cg0 + hb, cg1 + hb, c_len)
        ein = per_head(e_in, cg0, cg1, c_len)
        diff = (jnp.where(second, gc[:, cg1:cg1 + 1], gc[:, cg0:cg0 + 1])
                - jnp.where(second[0:1], gc_t[cg1:cg1 + 1, :], gc_t[cg0:cg0 + 1, :]))
        kbeta = k * b
        vb = (v_ref[rows, lanes].astype(F32) * b).astype(BF16)
        kbe = (kbeta * ein).astype(BF16)
        pairs.append(dict(
            pi=p, rows=rows, lanes=lanes, s_ref=s_ref, o_ref=o_ref, strict=strict,
            eye=jnp.where(row == col, 1.0, 0.0),
            gamma=jnp.where(incl, jnp.exp(jnp.where(incl, diff, 0.0)), 0.0),
            lhs1=jnp.concatenate([kbeta, q], axis=0).astype(BF16),
            k_bd=jnp.concatenate([jnp.concatenate([k16[:, :LANES], zeros], axis=1),
                                  jnp.concatenate([zeros, k16[:, LANES:]], axis=1)], axis=0),
            rhs_bd=jnp.concatenate(
                [jnp.concatenate([vb[:, :LANES], zeros, kbe[:, :LANES], zeros], axis=1),
                 jnp.concatenate([zeros, vb[:, LANES:], zeros, kbe[:, LANES:]], axis=1)], axis=0),
            qd=(q * ein).astype(BF16),
            kd=k * per_head(e_out, cg0, cg1, c_len),
            dl=per_head(d_last, cg0, cg1, 1)))
    return pairs


def _lane_group_masks(c_len):
    lane = lax.broadcasted_iota(jnp.int32, (c_len, LANES), 1)
    return [(lane >= g * c_len) & (lane < (g + 1) * c_len) for g in range(LANES // c_len)]


def _level_masks(c_len, width):
    r = lax.broadcasted_iota(jnp.int32, (c_len, width), 0)
    c = jnp.bitwise_and(lax.broadcasted_iota(jnp.int32, (c_len, width), 1), c_len - 1)

    def same(s):
        sh = s.bit_length() - 1
        return lax.shift_right_logical(r, sh) == lax.shift_right_logical(c, sh)

    masks = [same(INV_BASE)]
    s = INV_BASE
    while s < c_len:
        masks.append(same(2 * s) & jnp.logical_not(same(s)))
        s *= 2
    return masks


def _block_diag(x16, group_masks):
    c_len, width = x16.shape
    per_tile = LANES // c_len
    zeros = jnp.zeros((c_len, LANES), x16.dtype)
    blocks = []
    for i in range(width // c_len):
        t, g = divmod(i, per_tile)
        tile = jnp.where(group_masks[g], x16[:, t * LANES:(t + 1) * LANES], 0.0)
        blocks.append(jnp.concatenate([tile if tt == t else zeros for tt in range(width // LANES)], axis=1))
    return jnp.concatenate(blocks, axis=0)


def _delta_kernel(*refs, hb, scale, write_o):
    (qf, kf, vf, gf, qb, kb, vb, gb, par, s0f, s0b), outs = refs[:11], refs[11:]
    if write_o:
        of, ob, sf, sb = outs
    else:
        (sf, sb), of, ob = outs, None, None

    @pl.when(pl.program_id(2) == 0)
    def _():
        sf[...] = s0f[...]
        sb[...] = s0b[...]

    c_len = CHUNK
    n_sub = qf.shape[0] // c_len
    phases = []
    for j in range(n_sub):
        rf = slice(j * c_len, (j + 1) * c_len)
        rb = slice((n_sub - 1 - j) * c_len, (n_sub - j) * c_len)
        phases.append(_delta_units(qf, kf, vf, gf, par, sf, of, rows=rf, hb=hb, reverse=False, scale=scale)
                      + _delta_units(qb, kb, vb, gb, par, sb, ob, rows=rb, hb=hb, reverse=True, scale=scale))
    units = [u for ph in phases for u in ph]
    dn_t = (((1,), (1,)), ((), ()))

    for u in units:
        m1 = lax.dot_general(u["lhs1"], u["k_bd"], dn_t, preferred_element_type=F32)
        u["a"] = jnp.where(u["strict"], m1[:c_len] * u["gamma"], 0.0)
        u["attn"] = (m1[c_len:] * u["gamma"]).astype(BF16)
    quads = [dict(pairs=units[i:i + 2]) for i in range(0, len(units), 2)]
    group_masks = _lane_group_masks(c_len)
    lvl_masks = {}
    for qd in quads:
        a = jnp.concatenate([u["a"] for u in qd["pairs"]], axis=1)
        width = a.shape[1]
        if width not in lvl_masks:
            lvl_masks[width] = _level_masks(c_len, width)
        qd["same"] = group_masks
        same_base, qd["levels"] = lvl_masks[width][0], lvl_masks[width][1:]
        qd["a"] = a
        d = jnp.where(same_base, a, 0.0)
        qd["t"] = jnp.concatenate([u["eye"] for u in qd["pairs"]], axis=1) - d
        d16 = d.astype(BF16)
        qd["dpow"] = jnp.dot(d16, _block_diag(d16, qd["same"]), preferred_element_type=F32).astype(BF16)
    rounds = INV_BASE.bit_length() - 2
    for i in range(rounds):
        for qd in quads:
            bd = _block_diag(qd["dpow"], qd["same"])
            if i + 1 < rounds:
                res = jnp.dot(jnp.concatenate([qd["t"].astype(BF16), qd["dpow"]], axis=0), bd,
                              preferred_element_type=F32)
                qd["t"] = qd["t"] + res[:c_len]
                qd["dpow"] = res[c_len:].astype(BF16)
            else:
                qd["t"] = qd["t"] + jnp.dot(qd["t"].astype(BF16), bd, preferred_element_type=F32)
    for lvl in range(len(quads[0]["levels"])):
        for qd in quads:
            a_s = jnp.where(qd["levels"][lvl], qd["a"], 0.0).astype(BF16)
            qd["n"] = jnp.dot(a_s, _block_diag(qd["t"].astype(BF16), qd["same"]),
                              preferred_element_type=F32).astype(BF16)
        for qd in quads:
            qd["t"] = qd["t"] - jnp.dot(qd["t"].astype(BF16), _block_diag(qd["n"], qd["same"]),
                                        preferred_element_type=F32)
    for qd in quads:
        for j, u in enumerate(qd["pairs"]):
            t16 = qd["t"][:, 2 * j * c_len:2 * (j + 1) * c_len].astype(BF16)
            u["uw"] = jnp.dot(t16, u["rhs_bd"], preferred_element_type=F32)
    zeros = jnp.zeros((LANES, LANES), BF16)
    zc = jnp.zeros((c_len, LANES), BF16)
    for u in units:
        u["kd_t"] = jnp.transpose(
            jnp.concatenate([u["kd"][:, :LANES], u["kd"][:, LANES:]], axis=0)).astype(BF16)
    for phase in phases:
        for u in phase:
            s = u["s_ref"][u["pi"]]
            s16 = s.astype(BF16)
            s_bd = jnp.concatenate([jnp.concatenate([s16[:, :LANES], zeros], axis=1),
                                    jnp.concatenate([zeros, s16[:, LANES:]], axis=1)], axis=0)
            w = u["uw"][:, 2 * LANES:].astype(BF16)
            ws = jnp.dot(jnp.concatenate([w, u["qd"]], axis=0), s_bd, preferred_element_type=F32)
            vn = (u["uw"][:, :2 * LANES] - ws[:c_len]).astype(BF16)
            u["vn_bd"] = jnp.concatenate([jnp.concatenate([vn[:, :LANES], zc], axis=1),
                                          jnp.concatenate([zc, vn[:, LANES:]], axis=1)], axis=0)
            u["o"] = ws[c_len:]
            u["s"] = s
        for u in phase:
            if u["o_ref"] is None:
                ds = jnp.dot(u["kd_t"], u["vn_bd"], preferred_element_type=F32)
            else:
                res = jnp.dot(jnp.concatenate([u["attn"], u["kd_t"]], axis=0), u["vn_bd"],
                              preferred_element_type=F32)
                u["o_ref"][u["rows"], u["lanes"]] = (u["o"] + res[:c_len]).astype(u["o_ref"].dtype)
                ds = res[c_len:]
            u["s_ref"][u["pi"]] = u["s"] * u["dl"] + ds


def _delta(qkv, gates, par, s0f, s0b, seq_len, nh, write_o, hb=4, rows=4 * CHUNK):
    m = qkv.shape[0]
    bsz = m // seq_len
    rows = _tile(seq_len, rows)
    assert hb % 4 == 0 and rows % CHUNK == 0
    nc = seq_len // rows
    nhb = nh // hb
    w = hb * LANES
    fwd = lambda b, g, c: b * nc + c
    bwd = lambda b, g, c: b * nc + nc - 1 - c

    def qkv_specs(row):
        return [pl.BlockSpec((rows, w), lambda b, g, c, o=o: (row(b, g, c), o * nhb + g)) for o in range(3)]

    g_spec = lambda row: pl.BlockSpec((rows, LANES), lambda b, g, c: (row(b, g, c), g))
    s_spec = pl.BlockSpec((None, hb // 2, LANES, 2 * LANES), lambda b, g, c: (b, g, 0, 0))
    o_spec = lambda row: pl.BlockSpec((rows, w), lambda b, g, c: (row(b, g, c), g))
    s_shape = jax.ShapeDtypeStruct((bsz, nh // 2, LANES, 2 * LANES), F32)
    o_shape = jax.ShapeDtypeStruct((m, nh * LANES), BF16)
    out_specs = [s_spec, s_spec]
    out_shape = [s_shape, s_shape]
    if write_o:
        out_specs = [o_spec(fwd), o_spec(bwd)] + out_specs
        out_shape = [o_shape, o_shape] + out_shape
    return pl.pallas_call(
        functools.partial(_delta_kernel, hb=hb, scale=float(LANES) ** -0.5, write_o=write_o),
        grid=(bsz, nhb, nc),
        in_specs=qkv_specs(fwd) + [g_spec(fwd)] + qkv_specs(bwd) + [g_spec(bwd)]
        + [pl.BlockSpec((None, 8, LANES), lambda b, g, c: (g, 0, 0)), s_spec, s_spec],
        out_specs=out_specs,
        out_shape=out_shape,
        compiler_params=_params(("parallel", "parallel", "arbitrary")),
        name="delta_rule",
    )(qkv, qkv, qkv, gates, qkv, qkv, qkv, gates, par, s0f, s0b)


def _glu(v_ref, g_ref, rows):
    return v_ref[rows, :].astype(F32) * jax.nn.sigmoid(g_ref[rows, :].astype(F32))


def _conv_h_kernel(v_ref, g_ref, w_ref, o_ref, pad_ref):
    rows, tc = v_ref.shape
    half = CONF_K // 2
    lead = 16
    stride = lead + GRID_W + 16
    total = rows // GRID_W * stride
    w = w_ref[...]
    for g in range(rows // GRID_W):
        base = g * stride
        pad_ref[0, base:base + lead, :] = jnp.zeros((lead, tc), F32)
        pad_ref[0, base + lead + GRID_W:base + stride, :] = jnp.zeros((16, tc), F32)
        pad_ref[0, base + lead:base + lead + GRID_W, :] = _glu(v_ref, g_ref,
                                                               slice(g * GRID_W, (g + 1) * GRID_W))
    for s in range(1, SUBLANES):
        pad_ref[s, 0:total - SUBLANES, :] = pad_ref[0, s:total - SUBLANES + s, :]
    for g in range(rows // GRID_W):
        for cs in range(tc // LANES):
            lanes = slice(cs * LANES, (cs + 1) * LANES)
            acc = None
            for tap in range(CONF_K):
                off = lead - half + tap
                s = off % SUBLANES
                start = g * stride + off - s
                term = w[tap:tap + 1, lanes] * pad_ref[s, start:start + GRID_W, lanes]
                acc = term if acc is None else acc + term
            o_ref[g * GRID_W:(g + 1) * GRID_W, lanes] = acc.astype(o_ref.dtype)


def _conv_h(h_main, w_dw, col_v, col_g, half_c, rows=512, tc=256):
    m = h_main.shape[0]
    ov, og = col_v // tc, col_g // tc
    return pl.pallas_call(
        _conv_h_kernel,
        grid=(m // rows, half_c // tc),
        in_specs=[pl.BlockSpec((rows, tc), lambda i, j: (i, ov + j)),
                  pl.BlockSpec((rows, tc), lambda i, j: (i, og + j)),
                  pl.BlockSpec((CONF_K, tc), lambda i, j: (0, j))],
        out_specs=pl.BlockSpec((rows, tc), lambda i, j: (i, j)),
        out_shape=jax.ShapeDtypeStruct((m, half_c), BF16),
        scratch_shapes=[pltpu.VMEM((SUBLANES, rows // GRID_W * (GRID_W + 32), tc), F32)],
        compiler_params=_params(("parallel", "parallel")),
        name="conv_rows",
    )(h_main, h_main, w_dw)


def _conv_v_kernel(v_ref, g_ref, w_ref, o_ref, pad_ref):
    seq, tc = v_ref.shape
    halo = (CONF_K // 2) * GRID_W
    pad_ref[0:halo, :] = jnp.zeros((halo, tc), F32)
    pad_ref[halo + seq:halo + seq + halo, :] = jnp.zeros((halo, tc), F32)
    blk = 512

    def fill(i, carry):
        r0 = pl.multiple_of(i * blk, blk)
        pad_ref[pl.ds(halo + r0, blk), :] = _glu(v_ref, g_ref, pl.ds(r0, blk))
        return carry

    lax.fori_loop(0, seq // blk, fill, 0)
    w = w_ref[...]

    def body(i, carry):
        r0 = pl.multiple_of(i * GRID_W, GRID_W)
        for cs in range(tc // LANES):
            lanes = slice(cs * LANES, (cs + 1) * LANES)
            acc = None
            for tap in range(CONF_K):
                term = w[tap:tap + 1, lanes] * pad_ref[pl.ds(r0 + tap * GRID_W, GRID_W), lanes]
                acc = term if acc is None else acc + term
            o_ref[pl.ds(r0, GRID_W), lanes] = acc.astype(o_ref.dtype)
        return carry

    lax.fori_loop(0, seq // GRID_W, body, 0)


def _conv_v(h_main, w_dw, col_v, col_g, half_c, seq_len, tc=256):
    m = h_main.shape[0]
    ov, og = col_v // tc, col_g // tc
    ow = half_c // tc
    return pl.pallas_call(
        _conv_v_kernel,
        grid=(m // seq_len, half_c // tc),
        in_specs=[pl.BlockSpec((seq_len, tc), lambda b, j: (b, ov + j)),
                  pl.BlockSpec((seq_len, tc), lambda b, j: (b, og + j)),
                  pl.BlockSpec((CONF_K, tc), lambda b, j: (0, ow + j))],
        out_specs=pl.BlockSpec((seq_len, tc), lambda b, j: (b, j)),
        out_shape=jax.ShapeDtypeStruct((m, half_c), BF16),
        scratch_shapes=[pltpu.VMEM((seq_len + 2 * (CONF_K // 2) * GRID_W, tc), F32)],
        compiler_params=_params(("parallel", "parallel")),
        name="conv_cols",
    )(h_main, h_main, w_dw)


def _mix_kernel(of_ref, ob_ref, z_ref, yh_ref, yv_ref, gn_ref, bdw_ref, lg_ref, lb_ref, o_ref):
    dn = of_ref.shape[1]
    for h in range(dn // LANES):
        lanes = slice(h * LANES, (h + 1) * LANES)
        o = of_ref[:, lanes].astype(F32) + ob_ref[:, lanes].astype(F32)
        ms = jnp.mean(o * o, axis=-1, keepdims=True)
        on = o * lax.rsqrt(ms + 1e-6) * gn_ref[...]
        o_ref[:, lanes] = (on * _silu(z_ref[:, lanes].astype(F32))).astype(o_ref.dtype)
    y = jnp.concatenate([yh_ref[...].astype(F32), yv_ref[...].astype(F32)], axis=1) + bdw_ref[...]
    o_ref[:, dn:] = _silu(_ln(y, lg_ref[...], lb_ref[...])).astype(o_ref.dtype)


def _mix(o_f, o_b, h_main, z_blk, y_h, y_v, gn, bdw, lg, lb, tl=256):
    m, dn = o_f.shape
    half_c = y_h.shape[1]
    conf = 2 * half_c
    row = lambda w: pl.BlockSpec((tl, w), lambda i: (i, 0))
    vec = lambda w: pl.BlockSpec((1, w), lambda i: (0, 0))
    return pl.pallas_call(
        _mix_kernel,
        grid=(m // tl,),
        in_specs=[row(dn), row(dn), pl.BlockSpec((tl, dn), lambda i: (i, z_blk)), row(half_c), row(half_c),
                  vec(LANES), vec(conf), vec(conf), vec(conf)],
        out_specs=row(dn + conf),
        out_shape=jax.ShapeDtypeStruct((m, dn + conf), BF16),
        compiler_params=_params(("parallel",)),
        name="mixer_in",
    )(o_f, o_b, h_main, y_h, y_v, gn.reshape(1, LANES), bdw.reshape(1, conf), lg.reshape(1, conf),
      lb.reshape(1, conf))


def kernel(x, c, ctx, c_ctx, ln_in_g, ln_in_b, w_mod, b_mod, w_in, w_qkv_conv, a_log_f, dt_bias_f,
           a_log_b, dt_bias_b, dn_norm_g, conf_dw_w, conf_dw_b, conf_ln_g, conf_ln_b, w_out, ln1_g, ln1_b,
           w_mlp1, b_mlp1, w_mlp2, b_mlp2, ln2_g, ln2_b):
    assert w_mod.shape[0] == 1, "single-layer trunk only"
    bsz, seq, d = x.shape
    ctx_len = ctx.shape[1]
    nh = a_log_f.shape[1]
    dn = d // 2
    conf = d - dn
    assert dn == nh * LANES and seq % GRID_W == 0 and ctx_len % CHUNK == 0
    z_off, g_off, conf_off = 3 * dn, 4 * dn, 4 * dn + 4 * nh
    alpha = 2.0 ** 0.25
    hb = _tile(nh, 16)
    nhb = nh // hb
    m, mc = bsz * seq, bsz * ctx_len

    wi = w_in[0]
    w_qkvz = _cast_cols(w_in, 0, g_off)
    w_conf = _cast_cols(w_in, conf_off, 2 * conf)
    wg = wi[:, g_off:conf_off].reshape(d, 4, nhb, hb).transpose(0, 2, 1, 3).reshape(d, nhb, 4 * hb)
    wg = jnp.pad(wg, ((0, 0), (0, 0), (0, LANES - 4 * hb))).reshape(d, nhb * LANES).astype(BF16)
    zeros = jnp.zeros_like(a_log_f[0])
    par = jnp.stack([jnp.stack([a_log_f[0], zeros, a_log_b[0], zeros]),
                     jnp.stack([dt_bias_f[0], zeros, dt_bias_b[0], zeros])])
    par = par.reshape(2, 4, nhb, hb).transpose(2, 0, 1, 3).reshape(nhb, 2, 4 * hb)
    par = jnp.pad(par, ((0, 0), (0, 6), (0, LANES - 4 * hb)))
    w_o = w_out[0].astype(BF16)
    w1 = w_mlp1[0].astype(BF16)
    w2 = w_mlp2[0].astype(BF16)

    cc = jnp.concatenate([c, c_ctx[None, :], jnp.zeros((8 - bsz - 1, d), F32)], axis=0)
    mod3 = _mod_table(cc, w_mod[0], b_mod[0]).reshape(8, 1, 6 * d)

    tl = 256
    lat_row = lambda i: (i * tl) // seq
    ctx_row = lambda i: bsz

    uc = _ln_mod(ctx.reshape(mc, d), ln_in_g, ln_in_b, mod3, ctx_row, tl)
    hc = _matmul(uc, w_qkvz, BF16, n_cols=3 * dn)
    gates_c = _matmul(uc, wg, F32, tn=nhb * LANES)
    qkv_c = _prep(hc, w_qkv_conv[0], ctx_len, dn)
    s0 = jnp.zeros((bsz, nh // 2, LANES, 2 * LANES), F32)
    s_f, s_b = _delta(qkv_c, gates_c, par, s0, s0, ctx_len, nh, False, hb)

    xr = x.reshape(m, d)
    u0 = _ln_mod(xr, ln_in_g, ln_in_b, mod3, lat_row, tl)
    h_qkvz = _matmul(u0, w_qkvz, BF16)
    h_conf = _matmul(u0, w_conf, BF16)
    gates = _matmul(u0, wg, F32, tn=nhb * LANES)
    qkv = _prep(h_qkvz, w_qkv_conv[0], seq, dn)
    o_f, o_b, _, _ = _delta(qkv, gates, par, s_f, s_b, seq, nh, True, hb)
    y_h = _conv_h(h_conf, conf_dw_w[0], 0, conf, conf // 2)
    y_v = _conv_v(h_conf, conf_dw_w[0], conf // 2, conf + conf // 2, conf // 2, seq)
    cat = _mix(o_f, o_b, h_qkvz, z_off // dn, y_h, y_v, dn_norm_g[0], conf_dw_b[0], conf_ln_g[0],
               conf_ln_b[0], tl)
    y = _matmul(cat, w_o, BF16)
    u1, x1 = _mlp_in(xr, y, ln_in_g, ln_in_b, ln1_g[0], ln1_b[0], mod3, lat_row, alpha, tl)
    hid = _matmul(u1, w1, BF16, bias=b_mlp1[0])
    y2 = _matmul_k(hid, w2, BF16)
    out = _final(x1, y2, b_mlp2[0], ln2_g[0], ln2_b[0], mod3, lat_row, alpha, tl)
    return out.reshape(bsz, seq, d)
```

```python
import functools

import jax
import jax.numpy as jnp
from jax import lax
from jax.experimental import pallas as pl
from jax.experimental.pallas import tpu as pltpu

F32 = jnp.float32
BF16 = jnp.bfloat16

GRID_W = 64
CHUNK = 64
INV_BASE = 8
SHORT_CONV = 7
CONF_K = 31
LANES = 128
SUBLANES = 8
BF16_ROWS = 16
VMEM_LIMIT = 56 * 1024 * 1024
LN_EPS = 1e-5


def _params(sem):
    return pltpu.CompilerParams(dimension_semantics=sem, vmem_limit_bytes=VMEM_LIMIT)


def _tile(n, preferred):
    t = min(preferred, n)
    while n % t:
        t //= 2
    return t


def _silu(x):
    return x * jax.nn.sigmoid(x)


def _ln(x, g, b):
    mu = jnp.mean(x, axis=-1, keepdims=True)
    xc = x - mu
    var = jnp.mean(xc * xc, axis=-1, keepdims=True)
    return xc * lax.rsqrt(var + LN_EPS) * g + b


def _mod_kernel(c_ref, w_ref, b_ref, o_ref):
    s = _silu(c_ref[...]).astype(BF16)
    o_ref[...] = jnp.dot(s, w_ref[...].astype(BF16), preferred_element_type=F32) + b_ref[...]


def _mod_table(cc, w, b, tn=512):
    r, d = cc.shape
    n = w.shape[1]
    return pl.pallas_call(
        _mod_kernel,
        grid=(n // tn,),
        in_specs=[pl.BlockSpec((r, d), lambda j: (0, 0)),
                  pl.BlockSpec((d, tn), lambda j: (0, j)),
                  pl.BlockSpec((1, tn), lambda j: (0, j))],
        out_specs=pl.BlockSpec((r, tn), lambda j: (0, j)),
        out_shape=jax.ShapeDtypeStruct((r, n), F32),
        compiler_params=_params(("parallel",)),
        name="mod_table",
    )(cc, w, b.reshape(1, n))


def _mod_spec(d, row_of_tile, chunk):
    return pl.BlockSpec((None, 1, d), lambda i: (row_of_tile(i), 0, chunk))


def _ln_mod_kernel(x_ref, g_ref, b_ref, sh_ref, sc_ref, o_ref):
    xn = _ln(x_ref[...], g_ref[...], b_ref[...])
    o_ref[...] = (xn * (1.0 + sc_ref[...]) + sh_ref[...]).astype(o_ref.dtype)


def _ln_mod(x, g, b, mod3, row_of_tile, tl=256):
    m, d = x.shape
    row = pl.BlockSpec((tl, d), lambda i: (i, 0))
    vec = pl.BlockSpec((1, d), lambda i: (0, 0))
    return pl.pallas_call(
        _ln_mod_kernel,
        grid=(m // tl,),
        in_specs=[row, vec, vec, _mod_spec(d, row_of_tile, 0), _mod_spec(d, row_of_tile, 1)],
        out_specs=row,
        out_shape=jax.ShapeDtypeStruct((m, d), BF16),
        compiler_params=_params(("parallel",)),
        name="ln_mod",
    )(x, g.reshape(1, d), b.reshape(1, d), mod3, mod3)


def _mlp_in_kernel(x_ref, y_ref, gin_ref, bin_ref, ga_ref, g1_ref, b1_ref, sh_ref, sc_ref, u_ref, x1_ref,
                   *, alpha):
    xn = _ln(x_ref[...], gin_ref[...], bin_ref[...])
    x1 = _ln(alpha * xn + ga_ref[...] * y_ref[...].astype(F32), g1_ref[...], b1_ref[...])
    x1_ref[...] = x1
    u_ref[...] = (x1 * (1.0 + sc_ref[...]) + sh_ref[...]).astype(u_ref.dtype)


def _mlp_in(x, y, gin, bin_, g1, b1, mod3, row_of_tile, alpha, tl=256):
    m, d = x.shape
    row = pl.BlockSpec((tl, d), lambda i: (i, 0))
    vec = pl.BlockSpec((1, d), lambda i: (0, 0))
    v = lambda a: a.reshape(1, d)
    return pl.pallas_call(
        functools.partial(_mlp_in_kernel, alpha=alpha),
        grid=(m // tl,),
        in_specs=[row, row, vec, vec, _mod_spec(d, row_of_tile, 2), vec, vec,
                  _mod_spec(d, row_of_tile, 3), _mod_spec(d, row_of_tile, 4)],
        out_specs=[row, row],
        out_shape=[jax.ShapeDtypeStruct((m, d), BF16), jax.ShapeDtypeStruct((m, d), F32)],
        compiler_params=_params(("parallel",)),
        name="mlp_in",
    )(x, y, v(gin), v(bin_), mod3, v(g1), v(b1), mod3, mod3)


def _final_kernel(x1_ref, y2_ref, bm_ref, gm_ref, g2_ref, b2_ref, o_ref, *, alpha):
    y2 = y2_ref[...].astype(F32) + bm_ref[...]
    o_ref[...] = _ln(alpha * x1_ref[...] + gm_ref[...] * y2, g2_ref[...], b2_ref[...])


def _final(x1, y2, bm, g2, b2, mod3, row_of_tile, alpha, tl=256):
    m, d = x1.shape
    row = pl.BlockSpec((tl, d), lambda i: (i, 0))
    vec = pl.BlockSpec((1, d), lambda i: (0, 0))
    v = lambda a: a.reshape(1, d)
    return pl.pallas_call(
        functools.partial(_final_kernel, alpha=alpha),
        grid=(m // tl,),
        in_specs=[row, row, vec, _mod_spec(d, row_of_tile, 5), vec, vec],
        out_specs=row,
        out_shape=jax.ShapeDtypeStruct((m, d), F32),
        compiler_params=_params(("parallel",)),
        name="final_ln",
    )(x1, y2, v(bm), mod3, v(g2), v(b2))


def _mm_kernel(a_ref, w_ref, *rest, relu2, n_cast):
    rest = list(rest)
    b_ref = rest.pop(0) if relu2 else None
    cast_src, o_ref, cast_dst = rest[:n_cast], rest[n_cast], rest[n_cast + 1:]
    h = jnp.dot(a_ref[...], w_ref[...], preferred_element_type=F32)
    if relu2:
        r = jnp.maximum(h + b_ref[...], 0.0)
        h = r * r
    o_ref[...] = h.astype(o_ref.dtype)
    for src, dst in zip(cast_src, cast_dst):
        dst[...] = src[...].astype(dst.dtype)


def _matmul(a, w, out_dtype, n_cols=None, bias=None, cast=(), tm=1024, tn=1024):
    m, k = a.shape
    n = w.shape[1] if n_cols is None else n_cols
    tm, tn = _tile(m, tm), _tile(n, tn)
    gi, gj = m // tm, n // tn
    specs = [pl.BlockSpec((tm, k), lambda i, j: (i, 0)), pl.BlockSpec((k, tn), lambda i, j: (0, j))]
    args = (a, w)
    if bias is not None:
        specs.append(pl.BlockSpec((1, tn), lambda i, j: (0, j)))
        args += (bias.reshape(1, -1),)
    out_specs = [pl.BlockSpec((tm, tn), lambda i, j: (i, j))]
    out_shape = [jax.ShapeDtypeStruct((m, n), out_dtype)]
    for src in cast:
        rows, cols = src.shape
        assert rows % (gi * gj * BF16_ROWS) == 0
        slab = pl.BlockSpec((rows // (gi * gj), cols), lambda i, j: (i * gj + j, 0))
        specs.append(slab)
        args += (src,)
        out_specs.append(slab)
        out_shape.append(jax.ShapeDtypeStruct((rows, cols), BF16))
    res = pl.pallas_call(
        functools.partial(_mm_kernel, relu2=bias is not None, n_cast=len(cast)),
        grid=(gi, gj),
        in_specs=specs,
        out_specs=out_specs,
        out_shape=out_shape,
        compiler_params=_params(("parallel", "arbitrary")),
        name="matmul",
    )(*args)
    return res if cast else res[0]


def _mmk_kernel(a_ref, w_ref, o_ref, acc_ref):
    kk = pl.program_id(2)

    @pl.when(kk == 0)
    def _():
        acc_ref[...] = jnp.zeros_like(acc_ref)

    acc_ref[...] += jnp.dot(a_ref[...], w_ref[...], preferred_element_type=F32)

    @pl.when(kk == pl.num_programs(2) - 1)
    def _():
        o_ref[...] = acc_ref[...].astype(o_ref.dtype)


def _matmul_k(a, w, out_dtype, tm=1024, tn=2048, tk=2048):
    m, k = a.shape
    n = w.shape[1]
    tm, tn, tk = _tile(m, tm), _tile(n, tn), _tile(k, tk)
    return pl.pallas_call(
        _mmk_kernel,
        grid=(m // tm, n // tn, k // tk),
        in_specs=[pl.BlockSpec((tm, tk), lambda i, j, kk: (i, kk)),
                  pl.BlockSpec((tk, tn), lambda i, j, kk: (kk, j))],
        out_specs=pl.BlockSpec((tm, tn), lambda i, j, kk: (i, j)),
        out_shape=jax.ShapeDtypeStruct((m, n), out_dtype),
        scratch_shapes=[pltpu.VMEM((tm, tn), F32)],
        compiler_params=_params(("parallel", "parallel", "arbitrary")),
        name="matmul_k",
    )(a, w)


def _shift_matrix(blk, win):
    half = SHORT_CONV // 2
    cols = -(-SHORT_CONV * win // LANES) * LANES
    r = jnp.arange(blk)[:, None]
    c = jnp.arange(cols)[None, :]
    hit = c < 0
    for tap in range(SHORT_CONV):
        hit = hit | (c == tap * win + r + BF16_ROWS - half + tap)
    return hit.astype(BF16)


def _prep_kernel(prev_ref, main_ref, next_ref, w_ref, shift_ref, o_ref, *, tiles_per_seq, norm_blocks):
    i, j = pl.program_id(0), pl.program_id(1)
    t = i % tiles_per_seq
    tl, tc = main_ref.shape
    blk = min(LANES, tl)
    win = blk + 2 * BF16_ROWS
    prev = jnp.where(t > 0, prev_ref[...], 0.0)
    nxt = jnp.where(t < tiles_per_seq - 1, next_ref[...], 0.0)
    ext = jnp.concatenate([prev, main_ref[...], nxt], axis=0)
    w16 = w_ref[...].astype(BF16)
    scaled = [ext * w16[tap:tap + 1, :] for tap in range(SHORT_CONV)]
    shift = shift_ref[...]
    pad = jnp.zeros((shift.shape[1] - SHORT_CONV * win, tc), BF16)
    for b in range(tl // blk):
        rows = slice(b * blk, (b + 1) * blk)
        stacked = jnp.concatenate([z[b * blk:b * blk + win, :] for z in scaled] + [pad], axis=0)
        y = _silu(jnp.dot(shift, stacked, preferred_element_type=F32))
        for h in range(tc // LANES):
            lanes = slice(h * LANES, (h + 1) * LANES)
            yh = y[:, lanes]
            ss = jnp.sum(yh * yh, axis=-1, keepdims=True)
            yn = yh * lax.rsqrt(ss + 1e-6)
            o_ref[rows, lanes] = jnp.where(j < norm_blocks, yn, yh).astype(o_ref.dtype)


def _prep(h_main, w_conv, seq_len, dn, tl=512, tc=512):
    m = h_main.shape[0]
    tl = min(tl, seq_len)
    tiles_per_seq = seq_len // tl
    hb = tl // BF16_ROWS
    last = m // BF16_ROWS - 1
    blk = min(LANES, tl)
    shift = _shift_matrix(blk, blk + 2 * BF16_ROWS)
    return pl.pallas_call(
        functools.partial(_prep_kernel, tiles_per_seq=tiles_per_seq, norm_blocks=2 * dn // tc),
        grid=(m // tl, 3 * dn // tc),
        in_specs=[pl.BlockSpec((BF16_ROWS, tc), lambda i, j: (jnp.maximum(i * hb - 1, 0), j)),
                  pl.BlockSpec((tl, tc), lambda i, j: (i, j)),
                  pl.BlockSpec((BF16_ROWS, tc), lambda i, j: (jnp.minimum((i + 1) * hb, last), j)),
                  pl.BlockSpec((SHORT_CONV, tc), lambda i, j: (0, j)),
                  pl.BlockSpec(shift.shape, lambda i, j: (0, 0))],
        out_specs=pl.BlockSpec((tl, tc), lambda i, j: (i, j)),
        out_shape=jax.ShapeDtypeStruct((m, 3 * dn), BF16),
        compiler_params=_params(("parallel", "parallel")),
        name="prep_qkv",
    )(h_main, h_main, h_main, w_conv, shift)


def _cumsum_rows(x, reverse):
    n = x.shape[0]
    row = lax.broadcasted_iota(jnp.int32, x.shape, 0)
    s = 1
    while s < n:
        if reverse:
            x = x + jnp.where(row < n - s, pltpu.roll(x, n - s, axis=0), 0.0)
        else:
            x = x + jnp.where(row >= s, pltpu.roll(x, s, axis=0), 0.0)
        s *= 2
    return x


def _dot(a, b):
    return jnp.dot(a.astype(BF16), b.astype(BF16), preferred_element_type=F32)


def _delta_units(q_ref, k_ref, v_ref, g_ref, par_ref, s_ref, o_ref, *, rows, hb, reverse, scale):
    c_len = rows.stop - rows.start
    graw = g_ref[rows, :]
    z = graw + par_ref[1:2, :]
    softplus = jnp.maximum(z, 0.0) + jnp.log1p(jnp.exp(-jnp.abs(z)))
    gc = _cumsum_rows(-jnp.exp(par_ref[0:1, :]) * softplus, reverse)
    beta = jax.nn.sigmoid(graw)
    g_last = gc[0:1, :] if reverse else gc[c_len - 1:c_len, :]
    e_in = jnp.exp(gc)
    e_out = jnp.exp(g_last - gc)
    d_last = jnp.exp(g_last)
    gc_t = jnp.transpose(jnp.concatenate([gc, gc], axis=0))
    row = lax.broadcasted_iota(jnp.int32, (c_len, LANES), 0)
    lane = lax.broadcasted_iota(jnp.int32, (c_len, LANES), 1)
    second = lane >= c_len
    col = jnp.where(second, lane - c_len, lane)
    incl = (row <= col) if reverse else (row >= col)
    strict = (row < col) if reverse else (row > col)
    g_col0 = 2 * hb if reverse else 0
    zeros = jnp.zeros((c_len, LANES), BF16)

    def per_head(t, c0, c1, rows):
        return jnp.concatenate([jnp.broadcast_to(t[:, c0:c0 + 1], (rows, LANES)),
                                jnp.broadcast_to(t[:, c1:c1 + 1], (rows, LANES))], axis=1)

    pairs = []
    for p in range(hb // 2):
        cg0, cg1 = g_col0 + 2 * p, g_col0 + 2 * p + 1
        lanes = slice(2 * p * LANES, (2 * p + 2) * LANES)
        k16 = k_ref[rows, lanes]
        q = q_ref[rows, lanes].astype(F32) * scale
        k = k16.astype(F32)
        b = per_head(beta, cg0 + hb, cg1 + hb, c_len)
        ein = per_head(e_in, cg0, cg1, c_len)
        diff = (jnp.where(second, gc[:, cg1:cg1 + 1], gc[:, cg0:cg0 + 1])
                - jnp.where(second[0:1], gc_t[cg1:cg1 + 1, :], gc_t[cg0:cg0 + 1, :]))
        kbeta = k * b
        vb = (v_ref[rows, lanes].astype(F32) * b).astype(BF16)
        kbe = (kbeta * ein).astype(BF16)
        pairs.append(dict(
            pi=p, rows=rows, lanes=lanes, s_ref=s_ref, o_ref=o_ref, strict=strict,
            eye=jnp.where(row == col, 1.0, 0.0),
            gamma=jnp.where(incl, jnp.exp(jnp.where(incl, diff, 0.0)), 0.0),
            lhs1=jnp.concatenate([kbeta, q], axis=0).astype(BF16),
            k_bd=jnp.concatenate([jnp.concatenate([k16[:, :LANES], zeros], axis=1),
                                  jnp.concatenate([zeros, k16[:, LANES:]], axis=1)], axis=0),
            rhs_bd=jnp.concatenate(
                [jnp.concatenate([vb[:, :LANES], zeros, kbe[:, :LANES], zeros], axis=1),
                 jnp.concatenate([zeros, vb[:, LANES:], zeros, kbe[:, LANES:]], axis=1)], axis=0),
            qd=(q * ein).astype(BF16),
            kd=k * per_head(e_out, cg0, cg1, c_len),
            dl=per_head(d_last, cg0, cg1, 1)))
    return pairs


def _lane_group_masks(c_len):
    lane = lax.broadcasted_iota(jnp.int32, (c_len, LANES), 1)
    return [(lane >= g * c_len) & (lane < (g + 1) * c_len) for g in range(LANES // c_len)]


def _level_masks(c_len, width):
    r = lax.broadcasted_iota(jnp.int32, (c_len, width), 0)
    c = jnp.bitwise_and(lax.broadcasted_iota(jnp.int32, (c_len, width), 1), c_len - 1)

    def same(s):
        sh = s.bit_length() - 1
        return lax.shift_right_logical(r, sh) == lax.shift_right_logical(c, sh)

    masks = [same(INV_BASE)]
    s = INV_BASE
    while s < c_len:
        masks.append(same(2 * s) & jnp.logical_not(same(s)))
        s *= 2
    return masks


def _block_diag(x16, group_masks):
    c_len, width = x16.shape
    per_tile = LANES // c_len
    zeros = jnp.zeros((c_len, LANES), x16.dtype)
    blocks = []
    for i in range(width // c_len):
        t, g = divmod(i, per_tile)
        tile = jnp.where(group_masks[g], x16[:, t * LANES:(t + 1) * LANES], 0.0)
        blocks.append(jnp.concatenate([tile if tt == t else zeros for tt in range(width // LANES)], axis=1))
    return jnp.concatenate(blocks, axis=0)


def _delta_kernel(*refs, hb, scale, write_o):
    (qf, kf, vf, gf, qb, kb, vb, gb, par, s0f, s0b), outs = refs[:11], refs[11:]
    if write_o:
        of, ob, sf, sb = outs
    else:
        (sf, sb), of, ob = outs, None, None

    @pl.when(pl.program_id(2) == 0)
    def _():
        sf[...] = s0f[...]
        sb[...] = s0b[...]

    c_len = CHUNK
    n_sub = qf.shape[0] // c_len
    phases = []
    for j in range(n_sub):
        rf = slice(j * c_len, (j + 1) * c_len)
        rb = slice((n_sub - 1 - j) * c_len, (n_sub - j) * c_len)
        phases.append(_delta_units(qf, kf, vf, gf, par, sf, of, rows=rf, hb=hb, reverse=False, scale=scale)
                      + _delta_units(qb, kb, vb, gb, par, sb, ob, rows=rb, hb=hb, reverse=True, scale=scale))
    units = [u for ph in phases for u in ph]
    dn_t = (((1,), (1,)), ((), ()))

    for u in units:
        m1 = lax.dot_general(u["lhs1"], u["k_bd"], dn_t, preferred_element_type=F32)
        u["a"] = jnp.where(u["strict"], m1[:c_len] * u["gamma"], 0.0)
        u["attn"] = (m1[c_len:] * u["gamma"]).astype(BF16)
    quads = [dict(pairs=units[i:i + 2]) for i in range(0, len(units), 2)]
    group_masks = _lane_group_masks(c_len)
    lvl_masks = {}
    for qd in quads:
        a = jnp.concatenate([u["a"] for u in qd["pairs"]], axis=1)
        width = a.shape[1]
        if width not in lvl_masks:
            lvl_masks[width] = _level_masks(c_len, width)
        qd["same"] = group_masks
        same_base, qd["levels"] = lvl_masks[width][0], lvl_masks[width][1:]
        qd["a"] = a
        d = jnp.where(same_base, a, 0.0)
        qd["t"] = jnp.concatenate([u["eye"] for u in qd["pairs"]], axis=1) - d
        d16 = d.astype(BF16)
        qd["dpow"] = jnp.dot(d16, _block_diag(d16, qd["same"]), preferred_element_type=F32).astype(BF16)
    rounds = INV_BASE.bit_length() - 2
    for i in range(rounds):
        for qd in quads:
            bd = _block_diag(qd["dpow"], qd["same"])
            if i + 1 < rounds:
                res = jnp.dot(jnp.concatenate([qd["t"].astype(BF16), qd["dpow"]], axis=0), bd,
                              preferred_element_type=F32)
                qd["t"] = qd["t"] + res[:c_len]
                qd["dpow"] = res[c_len:].astype(BF16)
            else:
                qd["t"] = qd["t"] + jnp.dot(qd["t"].astype(BF16), bd, preferred_element_type=F32)
    for lvl in range(len(quads[0]["levels"])):
        for qd in quads:
            a_s = jnp.where(qd["levels"][lvl], qd["a"], 0.0).astype(BF16)
            qd["n"] = jnp.dot(a_s, _block_diag(qd["t"].astype(BF16), qd["same"]),
                              preferred_element_type=F32).astype(BF16)
        for qd in quads:
            qd["t"] = qd["t"] - jnp.dot(qd["t"].astype(BF16), _block_diag(qd["n"], qd["same"]),
                                        preferred_element_type=F32)
    for qd in quads:
        for j, u in enumerate(qd["pairs"]):
            t16 = qd["t"][:, 2 * j * c_len:2 * (j + 1) * c_len].astype(BF16)
            u["uw"] = jnp.dot(t16, u["rhs_bd"], preferred_element_type=F32)
    zeros = jnp.zeros((LANES, LANES), BF16)
    zc = jnp.zeros((c_len, LANES), BF16)
    for u in units:
        u["kd_t"] = jnp.transpose(
            jnp.concatenate([u["kd"][:, :LANES], u["kd"][:, LANES:]], axis=0)).astype(BF16)
    for phase in phases:
        for u in phase:
            s = u["s_ref"][u["pi"]]
            s16 = s.astype(BF16)
            s_bd = jnp.concatenate([jnp.concatenate([s16[:, :LANES], zeros], axis=1),
                                    jnp.concatenate([zeros, s16[:, LANES:]], axis=1)], axis=0)
            w = u["uw"][:, 2 * LANES:].astype(BF16)
            ws = jnp.dot(jnp.concatenate([w, u["qd"]], axis=0), s_bd, preferred_element_type=F32)
            vn = (u["uw"][:, :2 * LANES] - ws[:c_len]).astype(BF16)
            u["vn_bd"] = jnp.concatenate([jnp.concatenate([vn[:, :LANES], zc], axis=1),
                                          jnp.concatenate([zc, vn[:, LANES:]], axis=1)], axis=0)
            u["o"] = ws[c_len:]
            u["s"] = s
        for u in phase:
            if u["o_ref"] is None:
                ds = jnp.dot(u["kd_t"], u["vn_bd"], preferred_element_type=F32)
            else:
                res = jnp.dot(jnp.concatenate([u["attn"], u["kd_t"]], axis=0), u["vn_bd"],
                              preferred_element_type=F32)
                u["o_ref"][u["rows"], u["lanes"]] = (u["o"] + res[:c_len]).astype(u["o_ref"].dtype)
                ds = res[c_len:]
            u["s_ref"][u["pi"]] = u["s"] * u["dl"] + ds


def _delta(qkv, gates, par, s0f, s0b, seq_len, nh, write_o, hb=4, rows=4 * CHUNK):
    m = qkv.shape[0]
    bsz = m // seq_len
    rows = _tile(seq_len, rows)
    assert hb % 4 == 0 and rows % CHUNK == 0
    nc = seq_len // rows
    nhb = nh // hb
    w = hb * LANES
    fwd = lambda b, g, c: b * nc + c
    bwd = lambda b, g, c: b * nc + nc - 1 - c

    def qkv_specs(row):
        return [pl.BlockSpec((rows, w), lambda b, g, c, o=o: (row(b, g, c), o * nhb + g)) for o in range(3)]

    g_spec = lambda row: pl.BlockSpec((rows, LANES), lambda b, g, c: (row(b, g, c), g))
    s_spec = pl.BlockSpec((None, hb // 2, LANES, 2 * LANES), lambda b, g, c: (b, g, 0, 0))
    o_spec = lambda row: pl.BlockSpec((rows, w), lambda b, g, c: (row(b, g, c), g))
    s_shape = jax.ShapeDtypeStruct((bsz, nh // 2, LANES, 2 * LANES), F32)
    o_shape = jax.ShapeDtypeStruct((m, nh * LANES), BF16)
    out_specs = [s_spec, s_spec]
    out_shape = [s_shape, s_shape]
    if write_o:
        out_specs = [o_spec(fwd), o_spec(bwd)] + out_specs
        out_shape = [o_shape, o_shape] + out_shape
    return pl.pallas_call(
        functools.partial(_delta_kernel, hb=hb, scale=float(LANES) ** -0.5, write_o=write_o),
        grid=(bsz, nhb, nc),
        in_specs=qkv_specs(fwd) + [g_spec(fwd)] + qkv_specs(bwd) + [g_spec(bwd)]
        + [pl.BlockSpec((None, 8, LANES), lambda b, g, c: (g, 0, 0)), s_spec, s_spec],
        out_specs=out_specs,
        out_shape=out_shape,
        compiler_params=_params(("parallel", "parallel", "arbitrary")),
        name="delta_rule",
    )(qkv, qkv, qkv, gates, qkv, qkv, qkv, gates, par, s0f, s0b)


def _glu(v_ref, g_ref, rows):
    return v_ref[rows, :].astype(F32) * jax.nn.sigmoid(g_ref[rows, :].astype(F32))


def _conv_h_kernel(v_ref, g_ref, w_ref, o_ref, pad_ref):
    rows, tc = v_ref.shape
    half = CONF_K // 2
    lead = 16
    stride = lead + GRID_W + 16
    total = rows // GRID_W * stride
    w = w_ref[...]
    for g in range(rows // GRID_W):
        base = g * stride
        pad_ref[0, base:base + lead, :] = jnp.zeros((lead, tc), F32)
        pad_ref[0, base + lead + GRID_W:base + stride, :] = jnp.zeros((16, tc), F32)
        pad_ref[0, base + lead:base + lead + GRID_W, :] = _glu(v_ref, g_ref,
                                                               slice(g * GRID_W, (g + 1) * GRID_W))
    for s in range(1, SUBLANES):
        pad_ref[s, 0:total - SUBLANES, :] = pad_ref[0, s:total - SUBLANES + s, :]
    for g in range(rows // GRID_W):
        for cs in range(tc // LANES):
            lanes = slice(cs * LANES, (cs + 1) * LANES)
            acc = None
            for tap in range(CONF_K):
                off = lead - half + tap
                s = off % SUBLANES
                start = g * stride + off - s
                term = w[tap:tap + 1, lanes] * pad_ref[s, start:start + GRID_W, lanes]
                acc = term if acc is None else acc + term
            o_ref[g * GRID_W:(g + 1) * GRID_W, lanes] = acc.astype(o_ref.dtype)


def _conv_h(h_main, w_dw, col_v, col_g, half_c, rows=512, tc=256):
    m = h_main.shape[0]
    ov, og = col_v // tc, col_g // tc
    return pl.pallas_call(
        _conv_h_kernel,
        grid=(m // rows, half_c // tc),
        in_specs=[pl.BlockSpec((rows, tc), lambda i, j: (i, ov + j)),
                  pl.BlockSpec((rows, tc), lambda i, j: (i, og + j)),
                  pl.BlockSpec((CONF_K, tc), lambda i, j: (0, j))],
        out_specs=pl.BlockSpec((rows, tc), lambda i, j: (i, j)),
        out_shape=jax.ShapeDtypeStruct((m, half_c), BF16),
        scratch_shapes=[pltpu.VMEM((SUBLANES, rows // GRID_W * (GRID_W + 32), tc), F32)],
        compiler_params=_params(("parallel", "parallel")),
        name="conv_rows",
    )(h_main, h_main, w_dw)


def _conv_v_kernel(v_ref, g_ref, w_ref, o_ref, pad_ref):
    seq, tc = v_ref.shape
    halo = (CONF_K // 2) * GRID_W
    pad_ref[0:halo, :] = jnp.zeros((halo, tc), F32)
    pad_ref[halo + seq:halo + seq + halo, :] = jnp.zeros((halo, tc), F32)
    blk = 512

    def fill(i, carry):
        r0 = pl.multiple_of(i * blk, blk)
        pad_ref[pl.ds(halo + r0, blk), :] = _glu(v_ref, g_ref, pl.ds(r0, blk))
        return carry

    lax.fori_loop(0, seq // blk, fill, 0)
    w = w_ref[...]

    def body(i, carry):
        r0 = pl.multiple_of(i * GRID_W, GRID_W)
        for cs in range(tc // LANES):
            lanes = slice(cs * LANES, (cs + 1) * LANES)
            acc = None
            for tap in range(CONF_K):
                term = w[tap:tap + 1, lanes] * pad_ref[pl.ds(r0 + tap * GRID_W, GRID_W), lanes]
                acc = term if acc is None else acc + term
            o_ref[pl.ds(r0, GRID_W), lanes] = acc.astype(o_ref.dtype)
        return carry

    lax.fori_loop(0, seq // GRID_W, body, 0)


def _conv_v(h_main, w_dw, col_v, col_g, half_c, seq_len, tc=256):
    m = h_main.shape[0]
    ov, og = col_v // tc, col_g // tc
    ow = half_c // tc
    return pl.pallas_call(
        _conv_v_kernel,
        grid=(m // seq_len, half_c // tc),
        in_specs=[pl.BlockSpec((seq_len, tc), lambda b, j: (b, ov + j)),
                  pl.BlockSpec((seq_len, tc), lambda b, j: (b, og + j)),
                  pl.BlockSpec((CONF_K, tc), lambda b, j: (0, ow + j))],
        out_specs=pl.BlockSpec((seq_len, tc), lambda b, j: (b, j)),
        out_shape=jax.ShapeDtypeStruct((m, half_c), BF16),
        scratch_shapes=[pltpu.VMEM((seq_len + 2 * (CONF_K // 2) * GRID_W, tc), F32)],
        compiler_params=_params(("parallel", "parallel")),
        name="conv_cols",
    )(h_main, h_main, w_dw)


def _mix_kernel(of_ref, ob_ref, z_ref, yh_ref, yv_ref, gn_ref, bdw_ref, lg_ref, lb_ref, o_ref):
    dn = of_ref.shape[1]
    for h in range(dn // LANES):
        lanes = slice(h * LANES, (h + 1) * LANES)
        o = of_ref[:, lanes].astype(F32) + ob_ref[:, lanes].astype(F32)
        ms = jnp.mean(o * o, axis=-1, keepdims=True)
        on = o * lax.rsqrt(ms + 1e-6) * gn_ref[...]
        o_ref[:, lanes] = (on * _silu(z_ref[:, lanes].astype(F32))).astype(o_ref.dtype)
    y = jnp.concatenate([yh_ref[...].astype(F32), yv_ref[...].astype(F32)], axis=1) + bdw_ref[...]
    o_ref[:, dn:] = _silu(_ln(y, lg_ref[...], lb_ref[...])).astype(o_ref.dtype)


def _mix(o_f, o_b, h_main, z_blk, y_h, y_v, gn, bdw, lg, lb, tl=256):
    m, dn = o_f.shape
    half_c = y_h.shape[1]
    conf = 2 * half_c
    row = lambda w: pl.BlockSpec((tl, w), lambda i: (i, 0))
    vec = lambda w: pl.BlockSpec((1, w), lambda i: (0, 0))
    return pl.pallas_call(
        _mix_kernel,
        grid=(m // tl,),
        in_specs=[row(dn), row(dn), pl.BlockSpec((tl, dn), lambda i: (i, z_blk)), row(half_c), row(half_c),
                  vec(LANES), vec(conf), vec(conf), vec(conf)],
        out_specs=row(dn + conf),
        out_shape=jax.ShapeDtypeStruct((m, dn + conf), BF16),
        compiler_params=_params(("parallel",)),
        name="mixer_in",
    )(o_f, o_b, h_main, y_h, y_v, gn.reshape(1, LANES), bdw.reshape(1, conf), lg.reshape(1, conf),
      lb.reshape(1, conf))


def kernel(x, c, ctx, c_ctx, ln_in_g, ln_in_b, w_mod, b_mod, w_in, w_qkv_conv, a_log_f, dt_bias_f,
           a_log_b, dt_bias_b, dn_norm_g, conf_dw_w, conf_dw_b, conf_ln_g, conf_ln_b, w_out, ln1_g, ln1_b,
           w_mlp1, b_mlp1, w_mlp2, b_mlp2, ln2_g, ln2_b):
    assert w_mod.shape[0] == 1, "single-layer trunk only"
    bsz, seq, d = x.shape
    ctx_len = ctx.shape[1]
    nh = a_log_f.shape[1]
    dn = d // 2
    conf = d - dn
    assert dn == nh * LANES and seq % GRID_W == 0 and ctx_len % CHUNK == 0
    z_off, g_off, conf_off = 3 * dn, 4 * dn, 4 * dn + 4 * nh
    alpha = 2.0 ** 0.25
    hb = _tile(nh, 16)
    nhb = nh // hb
    m, mc = bsz * seq, bsz * ctx_len

    wi = w_in[0]
    w_qkvz = wi[:, :g_off].astype(BF16)
    w_conf = wi[:, conf_off:].astype(BF16)
    wg = wi[:, g_off:conf_off].reshape(d, 4, nhb, hb).transpose(0, 2, 1, 3).reshape(d, nhb, 4 * hb)
    wg = jnp.pad(wg, ((0, 0), (0, 0), (0, LANES - 4 * hb))).reshape(d, nhb * LANES).astype(BF16)
    zeros = jnp.zeros_like(a_log_f[0])
    par = jnp.stack([jnp.stack([a_log_f[0], zeros, a_log_b[0], zeros]),
                     jnp.stack([dt_bias_f[0], zeros, dt_bias_b[0], zeros])])
    par = par.reshape(2, 4, nhb, hb).transpose(2, 0, 1, 3).reshape(nhb, 2, 4 * hb)
    par = jnp.pad(par, ((0, 0), (0, 6), (0, LANES - 4 * hb)))

    cc = jnp.concatenate([c, c_ctx[None, :], jnp.zeros((8 - bsz - 1, d), F32)], axis=0)
    mod3 = _mod_table(cc, w_mod[0], b_mod[0]).reshape(8, 1, 6 * d)

    tl = 256
    lat_row = lambda i: (i * tl) // seq
    ctx_row = lambda i: bsz

    uc = _ln_mod(ctx.reshape(mc, d), ln_in_g, ln_in_b, mod3, ctx_row, tl)
    hc = _matmul(uc, w_qkvz, BF16, n_cols=3 * dn)
    gates_c = _matmul(uc, wg, F32, tn=nhb * LANES)
    qkv_c = _prep(hc, w_qkv_conv[0], ctx_len, dn)
    s0 = jnp.zeros((bsz, nh // 2, LANES, 2 * LANES), F32)
    s_f, s_b = _delta(qkv_c, gates_c, par, s0, s0, ctx_len, nh, False, hb)

    xr = x.reshape(m, d)
    u0 = _ln_mod(xr, ln_in_g, ln_in_b, mod3, lat_row, tl)
    h_qkvz, w1, w_o = _matmul(u0, w_qkvz, BF16, cast=(w_mlp1[0], w_out[0]))
    h_conf, w2 = _matmul(u0, w_conf, BF16, cast=(w_mlp2[0],))
    gates = _matmul(u0, wg, F32, tn=nhb * LANES)
    qkv = _prep(h_qkvz, w_qkv_conv[0], seq, dn)
    o_f, o_b, _, _ = _delta(qkv, gates, par, s_f, s_b, seq, nh, True, hb)
    y_h = _conv_h(h_conf, conf_dw_w[0], 0, conf, conf // 2)
    y_v = _conv_v(h_conf, conf_dw_w[0], conf // 2, conf + conf // 2, conf // 2, seq)
    cat = _mix(o_f, o_b, h_qkvz, z_off // dn, y_h, y_v, dn_norm_g[0], conf_dw_b[0], conf_ln_g[0],
               conf_ln_b[0], tl)
    y = _matmul(cat, w_o, BF16)
    u1, x1 = _mlp_in(xr, y, ln_in_g, ln_in_b, ln1_g[0], ln1_b[0], mod3, lat_row, alpha, tl)
    hid = _matmul(u1, w1, BF16, bias=b_mlp1[0])
    y2 = _matmul_k(hid, w2, BF16)
    out = _final(x1, y2, b_mlp2[0], ln2_g[0], ln2_b[0], mod3, lat_row, alpha, tl)
    return out.reshape(bsz, seq, d)
```

```python
import functools

import jax
import jax.numpy as jnp
from jax import lax
from jax.experimental import pallas as pl
from jax.experimental.pallas import tpu as pltpu

F32 = jnp.float32
BF16 = jnp.bfloat16

GRID_W = 64
CHUNK = 64
INV_BASE = 8
SHORT_CONV = 7
CONF_K = 31
LANES = 128
SUBLANES = 8
BF16_ROWS = 16
VMEM_LIMIT = 56 * 1024 * 1024
LN_EPS = 1e-5


def _params(sem):
    return pltpu.CompilerParams(dimension_semantics=sem, vmem_limit_bytes=VMEM_LIMIT)


def _tile(n, preferred):
    t = min(preferred, n)
    while n % t:
        t //= 2
    return t


def _silu(x):
    return x * jax.nn.sigmoid(x)


def _ln(x, g, b):
    mu = jnp.mean(x, axis=-1, keepdims=True)
    xc = x - mu
    var = jnp.mean(xc * xc, axis=-1, keepdims=True)
    return xc * lax.rsqrt(var + LN_EPS) * g + b


def _mod_kernel(c_ref, w_ref, b_ref, o_ref):
    s = _silu(c_ref[...]).astype(BF16)
    o_ref[...] = jnp.dot(s, w_ref[...].astype(BF16), preferred_element_type=F32) + b_ref[...]


def _mod_table(cc, w, b, tn=512):
    r, d = cc.shape
    n = w.shape[1]
    return pl.pallas_call(
        _mod_kernel,
        grid=(n // tn,),
        in_specs=[pl.BlockSpec((r, d), lambda j: (0, 0)),
                  pl.BlockSpec((d, tn), lambda j: (0, j)),
                  pl.BlockSpec((1, tn), lambda j: (0, j))],
        out_specs=pl.BlockSpec((r, tn), lambda j: (0, j)),
        out_shape=jax.ShapeDtypeStruct((r, n), F32),
        compiler_params=_params(("parallel",)),
        name="mod_table",
    )(cc, w, b.reshape(1, n))


def _mod_spec(d, row_of_tile, chunk):
    return pl.BlockSpec((None, 1, d), lambda i: (row_of_tile(i), 0, chunk))


def _ln_mod_kernel(x_ref, g_ref, b_ref, sh_ref, sc_ref, o_ref):
    xn = _ln(x_ref[...], g_ref[...], b_ref[...])
    o_ref[...] = (xn * (1.0 + sc_ref[...]) + sh_ref[...]).astype(o_ref.dtype)


def _ln_mod(x, g, b, mod3, row_of_tile, tl=256):
    m, d = x.shape
    row = pl.BlockSpec((tl, d), lambda i: (i, 0))
    vec = pl.BlockSpec((1, d), lambda i: (0, 0))
    return pl.pallas_call(
        _ln_mod_kernel,
        grid=(m // tl,),
        in_specs=[row, vec, vec, _mod_spec(d, row_of_tile, 0), _mod_spec(d, row_of_tile, 1)],
        out_specs=row,
        out_shape=jax.ShapeDtypeStruct((m, d), BF16),
        compiler_params=_params(("parallel",)),
        name="ln_mod",
    )(x, g.reshape(1, d), b.reshape(1, d), mod3, mod3)


def _mlp_in_kernel(x_ref, y_ref, gin_ref, bin_ref, ga_ref, g1_ref, b1_ref, sh_ref, sc_ref, u_ref, x1_ref,
                   *, alpha):
    xn = _ln(x_ref[...], gin_ref[...], bin_ref[...])
    x1 = _ln(alpha * xn + ga_ref[...] * y_ref[...].astype(F32), g1_ref[...], b1_ref[...])
    x1_ref[...] = x1
    u_ref[...] = (x1 * (1.0 + sc_ref[...]) + sh_ref[...]).astype(u_ref.dtype)


def _mlp_in(x, y, gin, bin_, g1, b1, mod3, row_of_tile, alpha, tl=256):
    m, d = x.shape
    row = pl.BlockSpec((tl, d), lambda i: (i, 0))
    vec = pl.BlockSpec((1, d), lambda i: (0, 0))
    v = lambda a: a.reshape(1, d)
    return pl.pallas_call(
        functools.partial(_mlp_in_kernel, alpha=alpha),
        grid=(m // tl,),
        in_specs=[row, row, vec, vec, _mod_spec(d, row_of_tile, 2), vec, vec,
                  _mod_spec(d, row_of_tile, 3), _mod_spec(d, row_of_tile, 4)],
        out_specs=[row, row],
        out_shape=[jax.ShapeDtypeStruct((m, d), BF16), jax.ShapeDtypeStruct((m, d), F32)],
        compiler_params=_params(("parallel",)),
        name="mlp_in",
    )(x, y, v(gin), v(bin_), mod3, v(g1), v(b1), mod3, mod3)


def _final_kernel(x1_ref, y2_ref, bm_ref, gm_ref, g2_ref, b2_ref, o_ref, *, alpha):
    y2 = y2_ref[...].astype(F32) + bm_ref[...]
    o_ref[...] = _ln(alpha * x1_ref[...] + gm_ref[...] * y2, g2_ref[...], b2_ref[...])


def _final(x1, y2, bm, g2, b2, mod3, row_of_tile, alpha, tl=256):
    m, d = x1.shape
    row = pl.BlockSpec((tl, d), lambda i: (i, 0))
    vec = pl.BlockSpec((1, d), lambda i: (0, 0))
    v = lambda a: a.reshape(1, d)
    return pl.pallas_call(
        functools.partial(_final_kernel, alpha=alpha),
        grid=(m // tl,),
        in_specs=[row, row, vec, _mod_spec(d, row_of_tile, 5), vec, vec],
        out_specs=row,
        out_shape=jax.ShapeDtypeStruct((m, d), F32),
        compiler_params=_params(("parallel",)),
        name="final_ln",
    )(x1, y2, v(bm), mod3, v(g2), v(b2))


def _mm_kernel(a_ref, w_ref, *rest, relu2, n_cast):
    rest = list(rest)
    b_ref = rest.pop(0) if relu2 else None
    cast_src, o_ref, cast_dst = rest[:n_cast], rest[n_cast], rest[n_cast + 1:]
    h = jnp.dot(a_ref[...], w_ref[...], preferred_element_type=F32)
    if relu2:
        r = jnp.maximum(h + b_ref[...], 0.0)
        h = r * r
    o_ref[...] = h.astype(o_ref.dtype)
    for src, dst in zip(cast_src, cast_dst):
        dst[...] = src[...].astype(dst.dtype)


def _matmul(a, w, out_dtype, n_cols=None, bias=None, cast=(), tm=1024, tn=1024):
    m, k = a.shape
    n = w.shape[1] if n_cols is None else n_cols
    tm, tn = _tile(m, tm), _tile(n, tn)
    gi, gj = m // tm, n // tn
    specs = [pl.BlockSpec((tm, k), lambda i, j: (i, 0)), pl.BlockSpec((k, tn), lambda i, j: (0, j))]
    args = (a, w)
    if bias is not None:
        specs.append(pl.BlockSpec((1, tn), lambda i, j: (0, j)))
        args += (bias.reshape(1, -1),)
    out_specs = [pl.BlockSpec((tm, tn), lambda i, j: (i, j))]
    out_shape = [jax.ShapeDtypeStruct((m, n), out_dtype)]
    for src in cast:
        rows, cols = src.shape
        assert rows % (gi * gj * BF16_ROWS) == 0
        slab = pl.BlockSpec((rows // (gi * gj), cols), lambda i, j: (i * gj + j, 0))
        specs.append(slab)
        args += (src,)
        out_specs.append(slab)
        out_shape.append(jax.ShapeDtypeStruct((rows, cols), BF16))
    res = pl.pallas_call(
        functools.partial(_mm_kernel, relu2=bias is not None, n_cast=len(cast)),
        grid=(gi, gj),
        in_specs=specs,
        out_specs=out_specs,
        out_shape=out_shape,
        compiler_params=_params(("parallel", "arbitrary")),
        name="matmul",
    )(*args)
    return res if cast else res[0]


def _mmk_kernel(a_ref, w_ref, o_ref, acc_ref):
    kk = pl.program_id(2)

    @pl.when(kk == 0)
    def _():
        acc_ref[...] = jnp.zeros_like(acc_ref)

    acc_ref[...] += jnp.dot(a_ref[...], w_ref[...], preferred_element_type=F32)

    @pl.when(kk == pl.num_programs(2) - 1)
    def _():
        o_ref[...] = acc_ref[...].astype(o_ref.dtype)


def _matmul_k(a, w, out_dtype, tm=1024, tn=2048, tk=2048):
    m, k = a.shape
    n = w.shape[1]
    tm, tn, tk = _tile(m, tm), _tile(n, tn), _tile(k, tk)
    return pl.pallas_call(
        _mmk_kernel,
        grid=(m // tm, n // tn, k // tk),
        in_specs=[pl.BlockSpec((tm, tk), lambda i, j, kk: (i, kk)),
                  pl.BlockSpec((tk, tn), lambda i, j, kk: (kk, j))],
        out_specs=pl.BlockSpec((tm, tn), lambda i, j, kk: (i, j)),
        out_shape=jax.ShapeDtypeStruct((m, n), out_dtype),
        scratch_shapes=[pltpu.VMEM((tm, tn), F32)],
        compiler_params=_params(("parallel", "parallel", "arbitrary")),
        name="matmul_k",
    )(a, w)


def _shift_matrix(blk, win):
    half = SHORT_CONV // 2
    cols = -(-SHORT_CONV * win // LANES) * LANES
    r = jnp.arange(blk)[:, None]
    c = jnp.arange(cols)[None, :]
    hit = c < 0
    for tap in range(SHORT_CONV):
        hit = hit | (c == tap * win + r + BF16_ROWS - half + tap)
    return hit.astype(BF16)


def _prep_kernel(prev_ref, main_ref, next_ref, w_ref, shift_ref, o_ref, *, tiles_per_seq, norm_blocks):
    i, j = pl.program_id(0), pl.program_id(1)
    t = i % tiles_per_seq
    tl, tc = main_ref.shape
    blk = min(LANES, tl)
    win = blk + 2 * BF16_ROWS
    prev = jnp.where(t > 0, prev_ref[...], 0.0)
    nxt = jnp.where(t < tiles_per_seq - 1, next_ref[...], 0.0)
    ext = jnp.concatenate([prev, main_ref[...], nxt], axis=0)
    w16 = w_ref[...].astype(BF16)
    scaled = [ext * w16[tap:tap + 1, :] for tap in range(SHORT_CONV)]
    shift = shift_ref[...]
    pad = jnp.zeros((shift.shape[1] - SHORT_CONV * win, tc), BF16)
    for b in range(tl // blk):
        rows = slice(b * blk, (b + 1) * blk)
        stacked = jnp.concatenate([z[b * blk:b * blk + win, :] for z in scaled] + [pad], axis=0)
        y = _silu(jnp.dot(shift, stacked, preferred_element_type=F32))
        for h in range(tc // LANES):
            lanes = slice(h * LANES, (h + 1) * LANES)
            yh = y[:, lanes]
            ss = jnp.sum(yh * yh, axis=-1, keepdims=True)
            yn = yh * lax.rsqrt(ss + 1e-6)
            o_ref[rows, lanes] = jnp.where(j < norm_blocks, yn, yh).astype(o_ref.dtype)


def _prep(h_main, w_conv, seq_len, dn, tl=512, tc=512):
    m = h_main.shape[0]
    tl = min(tl, seq_len)
    tiles_per_seq = seq_len // tl
    hb = tl // BF16_ROWS
    last = m // BF16_ROWS - 1
    blk = min(LANES, tl)
    shift = _shift_matrix(blk, blk + 2 * BF16_ROWS)
    return pl.pallas_call(
        functools.partial(_prep_kernel, tiles_per_seq=tiles_per_seq, norm_blocks=2 * dn // tc),
        grid=(m // tl, 3 * dn // tc),
        in_specs=[pl.BlockSpec((BF16_ROWS, tc), lambda i, j: (jnp.maximum(i * hb - 1, 0), j)),
                  pl.BlockSpec((tl, tc), lambda i, j: (i, j)),
                  pl.BlockSpec((BF16_ROWS, tc), lambda i, j: (jnp.minimum((i + 1) * hb, last), j)),
                  pl.BlockSpec((SHORT_CONV, tc), lambda i, j: (0, j)),
                  pl.BlockSpec(shift.shape, lambda i, j: (0, 0))],
        out_specs=pl.BlockSpec((tl, tc), lambda i, j: (i, j)),
        out_shape=jax.ShapeDtypeStruct((m, 3 * dn), BF16),
        compiler_params=_params(("parallel", "parallel")),
        name="prep_qkv",
    )(h_main, h_main, h_main, w_conv, shift)


def _cumsum_rows(x, reverse):
    n = x.shape[0]
    row = lax.broadcasted_iota(jnp.int32, x.shape, 0)
    s = 1
    while s < n:
        if reverse:
            x = x + jnp.where(row < n - s, pltpu.roll(x, n - s, axis=0), 0.0)
        else:
            x = x + jnp.where(row >= s, pltpu.roll(x, s, axis=0), 0.0)
        s *= 2
    return x


def _dot(a, b):
    return jnp.dot(a.astype(BF16), b.astype(BF16), preferred_element_type=F32)


def _delta_units(q_ref, k_ref, v_ref, g_ref, par_ref, s_ref, o_ref, *, rows, hb, reverse, scale):
    c_len = rows.stop - rows.start
    graw = g_ref[rows, :]
    z = graw + par_ref[1:2, :]
    softplus = jnp.maximum(z, 0.0) + jnp.log1p(jnp.exp(-jnp.abs(z)))
    gc = _cumsum_rows(-jnp.exp(par_ref[0:1, :]) * softplus, reverse)
    beta = jax.nn.sigmoid(graw)
    g_last = gc[0:1, :] if reverse else gc[c_len - 1:c_len, :]
    e_in = jnp.exp(gc)
    e_out = jnp.exp(g_last - gc)
    d_last = jnp.exp(g_last)
    gc_t = jnp.transpose(jnp.concatenate([gc, gc], axis=0))
    row = lax.broadcasted_iota(jnp.int32, (c_len, LANES), 0)
    lane = lax.broadcasted_iota(jnp.int32, (c_len, LANES), 1)
    second = lane >= c_len
    col = jnp.where(second, lane - c_len, lane)
    incl = (row <= col) if reverse else (row >= col)
    strict = (row < col) if reverse else (row > col)
    g_col0 = 2 * hb if reverse else 0
    zeros = jnp.zeros((c_len, LANES), BF16)

    def per_head(t, c0, c1, rows):
        return jnp.concatenate([jnp.broadcast_to(t[:, c0:c0 + 1], (rows, LANES)),
                                jnp.broadcast_to(t[:, c1:c1 + 1], (rows, LANES))], axis=1)

    pairs = []
    for p in range(hb // 2):
        cg0, cg1 = g_col0 + 2 * p, g_col0 + 2 * p + 1
        lanes = slice(2 * p * LANES, (2 * p + 2) * LANES)
        k16 = k_ref[rows, lanes]
        q = q_ref[rows, lanes].astype(F32) * scale
        k = k16.astype(F32)
        b = per_head(beta, cg0 + hb, cg1 + hb, c_len)
        ein = per_head(e_in, cg0, cg1, c_len)
        diff = (jnp.where(second, gc[:, cg1:cg1 + 1], gc[:, cg0:cg0 + 1])
                - jnp.where(second[0:1], gc_t[cg1:cg1 + 1, :], gc_t[cg0:cg0 + 1, :]))
        kbeta = k * b
        vb = (v_ref[rows, lanes].astype(F32) * b).astype(BF16)
        kbe = (kbeta * ein).astype(BF16)
        pairs.append(dict(
            pi=p, rows=rows, lanes=lanes, s_ref=s_ref, o_ref=o_ref, strict=strict,
            eye=jnp.where(row == col, 1.0, 0.0),
            gamma=jnp.where(incl, jnp.exp(jnp.where(incl, diff, 0.0)), 0.0),
            lhs1=jnp.concatenate([kbeta, q], axis=0).astype(BF16),
            k_bd=jnp.concatenate([jnp.concatenate([k16[:, :LANES], zeros], axis=1),
                                  jnp.concatenate([zeros, k16[:, LANES:]], axis=1)], axis=0),
            rhs_bd=jnp.concatenate(
                [jnp.concatenate([vb[:, :LANES], zeros, kbe[:, :LANES], zeros], axis=1),
                 jnp.concatenate([zeros, vb[:, LANES:], zeros, kbe[:, LANES:]], axis=1)], axis=0),
            qd=(q * ein).astype(BF16),
            kd=k * per_head(e_out, cg0, cg1, c_len),
            dl=per_head(d_last, cg0, cg1, 1)))
    return pairs


def _lane_group_masks(c_len):
    lane = lax.broadcasted_iota(jnp.int32, (c_len, LANES), 1)
    return [(lane >= g * c_len) & (lane < (g + 1) * c_len) for g in range(LANES // c_len)]


def _level_masks(c_len, width):
    r = lax.broadcasted_iota(jnp.int32, (c_len, width), 0)
    c = jnp.bitwise_and(lax.broadcasted_iota(jnp.int32, (c_len, width), 1), c_len - 1)

    def same(s):
        sh = s.bit_length() - 1
        return lax.shift_right_logical(r, sh) == lax.shift_right_logical(c, sh)

    masks = [same(INV_BASE)]
    s = INV_BASE
    while s < c_len:
        masks.append(same(2 * s) & jnp.logical_not(same(s)))
        s *= 2
    return masks


def _block_diag(x16, group_masks):
    c_len, width = x16.shape
    per_tile = LANES // c_len
    zeros = jnp.zeros((c_len, LANES), x16.dtype)
    blocks = []
    for i in range(width // c_len):
        t, g = divmod(i, per_tile)
        tile = jnp.where(group_masks[g], x16[:, t * LANES:(t + 1) * LANES], 0.0)
        blocks.append(jnp.concatenate([tile if tt == t else zeros for tt in range(width // LANES)], axis=1))
    return jnp.concatenate(blocks, axis=0)


def _delta_kernel(*refs, hb, scale, write_o):
    (qf, kf, vf, gf, qb, kb, vb, gb, par, s0f, s0b), outs = refs[:11], refs[11:]
    if write_o:
        of, ob, sf, sb = outs
    else:
        (sf, sb), of, ob = outs, None, None

    @pl.when(pl.program_id(2) == 0)
    def _():
        sf[...] = s0f[...]
        sb[...] = s0b[...]

    c_len = CHUNK
    n_sub = qf.shape[0] // c_len
    phases = []
    for j in range(n_sub):
        rf = slice(j * c_len, (j + 1) * c_len)
        rb = slice((n_sub - 1 - j) * c_len, (n_sub - j) * c_len)
        phases.append(_delta_units(qf, kf, vf, gf, par, sf, of, rows=rf, hb=hb, reverse=False, scale=scale)
                      + _delta_units(qb, kb, vb, gb, par, sb, ob, rows=rb, hb=hb, reverse=True, scale=scale))
    units = [u for ph in phases for u in ph]
    dn_t = (((1,), (1,)), ((), ()))

    for u in units:
        m1 = lax.dot_general(u["lhs1"], u["k_bd"], dn_t, preferred_element_type=F32)
        u["a"] = jnp.where(u["strict"], m1[:c_len] * u["gamma"], 0.0)
        u["attn"] = (m1[c_len:] * u["gamma"]).astype(BF16)
    quads = [dict(pairs=units[i:i + 2]) for i in range(0, len(units), 2)]
    group_masks = _lane_group_masks(c_len)
    lvl_masks = {}
    for qd in quads:
        a = jnp.concatenate([u["a"] for u in qd["pairs"]], axis=1)
        width = a.shape[1]
        if width not in lvl_masks:
            lvl_masks[width] = _level_masks(c_len, width)
        qd["same"] = group_masks
        same_base, qd["levels"] = lvl_masks[width][0], lvl_masks[width][1:]
        qd["a"] = a
        d = jnp.where(same_base, a, 0.0)
        qd["t"] = jnp.concatenate([u["eye"] for u in qd["pairs"]], axis=1) - d
        d16 = d.astype(BF16)
        qd["dpow"] = jnp.dot(d16, _block_diag(d16, qd["same"]), preferred_element_type=F32).astype(BF16)
    rounds = INV_BASE.bit_length() - 2
    for i in range(rounds):
        for qd in quads:
            bd = _block_diag(qd["dpow"], qd["same"])
            if i + 1 < rounds:
                res = jnp.dot(jnp.concatenate([qd["t"].astype(BF16), qd["dpow"]], axis=0), bd,
                              preferred_element_type=F32)
                qd["t"] = qd["t"] + res[:c_len]
                qd["dpow"] = res[c_len:].astype(BF16)
            else:
                qd["t"] = qd["t"] + jnp.dot(qd["t"].astype(BF16), bd, preferred_element_type=F32)
    for lvl in range(len(quads[0]["levels"])):
        for qd in quads:
            a_s = jnp.where(qd["levels"][lvl], qd["a"], 0.0).astype(BF16)
            qd["n"] = jnp.dot(a_s, _block_diag(qd["t"].astype(BF16), qd["same"]),
                              preferred_element_type=F32).astype(BF16)
        for qd in quads:
            qd["t"] = qd["t"] - jnp.dot(qd["t"].astype(BF16), _block_diag(qd["n"], qd["same"]),
                                        preferred_element_type=F32)
    for qd in quads:
        for j, u in enumerate(qd["pairs"]):
            t16 = qd["t"][:, 2 * j * c_len:2 * (j + 1) * c_len].astype(BF16)
            u["uw"] = jnp.dot(t16, u["rhs_bd"], preferred_element_type=F32)
    zeros = jnp.zeros((LANES, LANES), BF16)
    zc = jnp.zeros((c_len, LANES), BF16)
    for u in units:
        u["kd_t"] = jnp.transpose(
            jnp.concatenate([u["kd"][:, :LANES], u["kd"][:, LANES:]], axis=0)).astype(BF16)
    for phase in phases:
        for u in phase:
            s = u["s_ref"][u["pi"]]
            s16 = s.astype(BF16)
            s_bd = jnp.concatenate([jnp.concatenate([s16[:, :LANES], zeros], axis=1),
                                    jnp.concatenate([zeros, s16[:, LANES:]], axis=1)], axis=0)
            w = u["uw"][:, 2 * LANES:].astype(BF16)
            ws = jnp.dot(jnp.concatenate([w, u["qd"]], axis=0), s_bd, preferred_element_type=F32)
            vn = (u["uw"][:, :2 * LANES] - ws[:c_len]).astype(BF16)
            u["vn_bd"] = jnp.concatenate([jnp.concatenate([vn[:, :LANES], zc], axis=1),
                                          jnp.concatenate([zc, vn[:, LANES:]], axis=1)], axis=0)
            u["o"] = ws[c_len:]
            u["s"] = s
        for u in phase:
            if u["o_ref"] is None:
                ds = jnp.dot(u["kd_t"], u["vn_bd"], preferred_element_type=F32)
            else:
                res = jnp.dot(jnp.concatenate([u["attn"], u["kd_t"]], axis=0), u["vn_bd"],
                              preferred_element_type=F32)
                u["o_ref"][u["rows"], u["lanes"]] = (u["o"] + res[:c_len]).astype(u["o_ref"].dtype)
                ds = res[c_len:]
            u["s_ref"][u["pi"]] = u["s"] * u["dl"] + ds


def _delta(qkv, gates, par, s0f, s0b, seq_len, nh, write_o, hb=4, rows=4 * CHUNK):
    m = qkv.shape[0]
    bsz = m // seq_len
    rows = _tile(seq_len, rows)
    assert hb % 4 == 0 and rows % CHUNK == 0
    nc = seq_len // rows
    nhb = nh // hb
    w = hb * LANES
    fwd = lambda b, g, c: b * nc + c
    bwd = lambda b, g, c: b * nc + nc - 1 - c

    def qkv_specs(row):
        return [pl.BlockSpec((rows, w), lambda b, g, c, o=o: (row(b, g, c), o * nhb + g)) for o in range(3)]

    g_spec = lambda row: pl.BlockSpec((rows, LANES), lambda b, g, c: (row(b, g, c), g))
    s_spec = pl.BlockSpec((None, hb // 2, LANES, 2 * LANES), lambda b, g, c: (b, g, 0, 0))
    o_spec = lambda row: pl.BlockSpec((rows, w), lambda b, g, c: (row(b, g, c), g))
    s_shape = jax.ShapeDtypeStruct((bsz, nh // 2, LANES, 2 * LANES), F32)
    o_shape = jax.ShapeDtypeStruct((m, nh * LANES), BF16)
    out_specs = [s_spec, s_spec]
    out_shape = [s_shape, s_shape]
    if write_o:
        out_specs = [o_spec(fwd), o_spec(bwd)] + out_specs
        out_shape = [o_shape, o_shape] + out_shape
    return pl.pallas_call(
        functools.partial(_delta_kernel, hb=hb, scale=float(LANES) ** -0.5, write_o=write_o),
        grid=(bsz, nhb, nc),
        in_specs=qkv_specs(fwd) + [g_spec(fwd)] + qkv_specs(bwd) + [g_spec(bwd)]
        + [pl.BlockSpec((None, 8, LANES), lambda b, g, c: (g, 0, 0)), s_spec, s_spec],
        out_specs=out_specs,
        out_shape=out_shape,
        compiler_params=_params(("parallel", "parallel", "arbitrary")),
        name="delta_rule",
    )(qkv, qkv, qkv, gates, qkv, qkv, qkv, gates, par, s0f, s0b)


def _glu(v_ref, g_ref, rows):
    return v_ref[rows, :].astype(F32) * jax.nn.sigmoid(g_ref[rows, :].astype(F32))


def _conv_h_kernel(v_ref, g_ref, w_ref, o_ref, pad_ref):
    rows, tc = v_ref.shape
    half = CONF_K // 2
    lead = 16
    stride = lead + GRID_W + 16
    total = rows // GRID_W * stride
    w = w_ref[...]
    for g in range(rows // GRID_W):
        base = g * stride
        pad_ref[0, base:base + lead, :] = jnp.zeros((lead, tc), F32)
        pad_ref[0, base + lead + GRID_W:base + stride, :] = jnp.zeros((16, tc), F32)
        pad_ref[0, base + lead:base + lead + GRID_W, :] = _glu(v_ref, g_ref,
                                                               slice(g * GRID_W, (g + 1) * GRID_W))
    for s in range(1, SUBLANES):
        pad_ref[s, 0:total - SUBLANES, :] = pad_ref[0, s:total - SUBLANES + s, :]
    for g in range(rows // GRID_W):
        for cs in range(tc // LANES):
            lanes = slice(cs * LANES, (cs + 1) * LANES)
            acc = None
            for tap in range(CONF_K):
                off = lead - half + tap
                s = off % SUBLANES
                start = g * stride + off - s
                term = w[tap:tap + 1, lanes] * pad_ref[s, start:start + GRID_W, lanes]
                acc = term if acc is None else acc + term
            o_ref[g * GRID_W:(g + 1) * GRID_W, lanes] = acc.astype(o_ref.dtype)


def _conv_h(h_main, w_dw, col_v, col_g, half_c, rows=512, tc=256):
    m = h_main.shape[0]
    ov, og = col_v // tc, col_g // tc
    return pl.pallas_call(
        _conv_h_kernel,
        grid=(m // rows, half_c // tc),
        in_specs=[pl.BlockSpec((rows, tc), lambda i, j: (i, ov + j)),
                  pl.BlockSpec((rows, tc), lambda i, j: (i, og + j)),
                  pl.BlockSpec((CONF_K, tc), lambda i, j: (0, j))],
        out_specs=pl.BlockSpec((rows, tc), lambda i, j: (i, j)),
        out_shape=jax.ShapeDtypeStruct((m, half_c), BF16),
        scratch_shapes=[pltpu.VMEM((SUBLANES, rows // GRID_W * (GRID_W + 32), tc), F32)],
        compiler_params=_params(("parallel", "parallel")),
        name="conv_rows",
    )(h_main, h_main, w_dw)


def _conv_v_kernel(v_ref, g_ref, w_ref, o_ref, pad_ref):
    seq, tc = v_ref.shape
    halo = (CONF_K // 2) * GRID_W
    pad_ref[0:halo, :] = jnp.zeros((halo, tc), F32)
    pad_ref[halo + seq:halo + seq + halo, :] = jnp.zeros((halo, tc), F32)
    blk = 512

    def fill(i, carry):
        r0 = pl.multiple_of(i * blk, blk)
        pad_ref[pl.ds(halo + r0, blk), :] = _glu(v_ref, g_ref, pl.ds(r0, blk))
        return carry

    lax.fori_loop(0, seq // blk, fill, 0)
    w = w_ref[...]

    def body(i, carry):
        r0 = pl.multiple_of(i * GRID_W, GRID_W)
        for cs in range(tc // LANES):
            lanes = slice(cs * LANES, (cs + 1) * LANES)
            acc = None
            for tap in range(CONF_K):
                term = w[tap:tap + 1, lanes] * pad_ref[pl.ds(r0 + tap * GRID_W, GRID_W), lanes]
                acc = term if acc is None else acc + term
            o_ref[pl.ds(r0, GRID_W), lanes] = acc.astype(o_ref.dtype)
        return carry

    lax.fori_loop(0, seq // GRID_W, body, 0)


def _conv_v(h_main, w_dw, col_v, col_g, half_c, seq_len, tc=256):
    m = h_main.shape[0]
    ov, og = col_v // tc, col_g // tc
    ow = half_c // tc
    return pl.pallas_call(
        _conv_v_kernel,
        grid=(m // seq_len, half_c // tc),
        in_specs=[pl.BlockSpec((seq_len, tc), lambda b, j: (b, ov + j)),
                  pl.BlockSpec((seq_len, tc), lambda b, j: (b, og + j)),
                  pl.BlockSpec((CONF_K, tc), lambda b, j: (0, ow + j))],
        out_specs=pl.BlockSpec((seq_len, tc), lambda b, j: (b, j)),
        out_shape=jax.ShapeDtypeStruct((m, half_c), BF16),
        scratch_shapes=[pltpu.VMEM((seq_len + 2 * (CONF_K // 2) * GRID_W, tc), F32)],
        compiler_params=_params(("parallel", "parallel")),
        name="conv_cols",
    )(h_main, h_main, w_dw)


def _mix_kernel(of_ref, ob_ref, z_ref, yh_ref, yv_ref, gn_ref, bdw_ref, lg_ref, lb_ref, o_ref):
    dn = of_ref.shape[1]
    for h in range(dn // LANES):
        lanes = slice(h * LANES, (h + 1) * LANES)
        o = of_ref[:, lanes].astype(F32) + ob_ref[:, lanes].astype(F32)
        ms = jnp.mean(o * o, axis=-1, keepdims=True)
        on = o * lax.rsqrt(ms + 1e-6) * gn_ref[...]
        o_ref[:, lanes] = (on * _silu(z_ref[:, lanes].astype(F32))).astype(o_ref.dtype)
    y = jnp.concatenate([yh_ref[...].astype(F32), yv_ref[...].astype(F32)], axis=1) + bdw_ref[...]
    o_ref[:, dn:] = _silu(_ln(y, lg_ref[...], lb_ref[...])).astype(o_ref.dtype)


def _mix(o_f, o_b, h_main, z_blk, y_h, y_v, gn, bdw, lg, lb, tl=256):
    m, dn = o_f.shape
    half_c = y_h.shape[1]
    conf = 2 * half_c
    row = lambda w: pl.BlockSpec((tl, w), lambda i: (i, 0))
    vec = lambda w: pl.BlockSpec((1, w), lambda i: (0, 0))
    return pl.pallas_call(
        _mix_kernel,
        grid=(m // tl,),
        in_specs=[row(dn), row(dn), pl.BlockSpec((tl, dn), lambda i: (i, z_blk)), row(half_c), row(half_c),
                  vec(LANES), vec(conf), vec(conf), vec(conf)],
        out_specs=row(dn + conf),
        out_shape=jax.ShapeDtypeStruct((m, dn + conf), BF16),
        compiler_params=_params(("parallel",)),
        name="mixer_in",
    )(o_f, o_b, h_main, y_h, y_v, gn.reshape(1, LANES), bdw.reshape(1, conf), lg.reshape(1, conf),
      lb.reshape(1, conf))


def kernel(x, c, ctx, c_ctx, ln_in_g, ln_in_b, w_mod, b_mod, w_in, w_qkv_conv, a_log_f, dt_bias_f,
           a_log_b, dt_bias_b, dn_norm_g, conf_dw_w, conf_dw_b, conf_ln_g, conf_ln_b, w_out, ln1_g, ln1_b,
           w_mlp1, b_mlp1, w_mlp2, b_mlp2, ln2_g, ln2_b):
    assert w_mod.shape[0] == 1, "single-layer trunk only"
    bsz, seq, d = x.shape
    ctx_len = ctx.shape[1]
    nh = a_log_f.shape[1]
    dn = d // 2
    conf = d - dn
    assert dn == nh * LANES and seq % GRID_W == 0 and ctx_len % CHUNK == 0
    z_off, g_off, conf_off = 3 * dn, 4 * dn, 4 * dn + 4 * nh
    alpha = 2.0 ** 0.25
    hb = _tile(nh, 16)
    nhb = nh // hb
    m, mc = bsz * seq, bsz * ctx_len

    wi = w_in[0]
    w_in16 = wi.astype(BF16)
    w_conf = w_in16[:, conf_off:]
    wg = wi[:, g_off:conf_off].reshape(d, 4, nhb, hb).transpose(0, 2, 1, 3).reshape(d, nhb, 4 * hb)
    wg = jnp.pad(wg, ((0, 0), (0, 0), (0, LANES - 4 * hb))).reshape(d, nhb * LANES).astype(BF16)
    zeros = jnp.zeros_like(a_log_f[0])
    par = jnp.stack([jnp.stack([a_log_f[0], zeros, a_log_b[0], zeros]),
                     jnp.stack([dt_bias_f[0], zeros, dt_bias_b[0], zeros])])
    par = par.reshape(2, 4, nhb, hb).transpose(2, 0, 1, 3).reshape(nhb, 2, 4 * hb)
    par = jnp.pad(par, ((0, 0), (0, 6), (0, LANES - 4 * hb)))

    cc = jnp.concatenate([c, c_ctx[None, :], jnp.zeros((8 - bsz - 1, d), F32)], axis=0)
    mod3 = _mod_table(cc, w_mod[0], b_mod[0]).reshape(8, 1, 6 * d)

    tl = 256
    lat_row = lambda i: (i * tl) // seq
    ctx_row = lambda i: bsz

    uc = _ln_mod(ctx.reshape(mc, d), ln_in_g, ln_in_b, mod3, ctx_row, tl)
    hc = _matmul(uc, w_in16, BF16, n_cols=3 * dn)
    gates_c = _matmul(uc, wg, F32, tn=nhb * LANES)
    qkv_c = _prep(hc, w_qkv_conv[0], ctx_len, dn)
    s0 = jnp.zeros((bsz, nh // 2, LANES, 2 * LANES), F32)
    s_f, s_b = _delta(qkv_c, gates_c, par, s0, s0, ctx_len, nh, False, hb)

    xr = x.reshape(m, d)
    u0 = _ln_mod(xr, ln_in_g, ln_in_b, mod3, lat_row, tl)
    h_qkvz, w1, w_o = _matmul(u0, w_in16, BF16, n_cols=g_off, cast=(w_mlp1[0], w_out[0]))
    h_conf, w2 = _matmul(u0, w_conf, BF16, cast=(w_mlp2[0],))
    gates = _matmul(u0, wg, F32, tn=nhb * LANES)
    qkv = _prep(h_qkvz, w_qkv_conv[0], seq, dn)
    o_f, o_b, _, _ = _delta(qkv, gates, par, s_f, s_b, seq, nh, True, hb)
    y_h = _conv_h(h_conf, conf_dw_w[0], 0, conf, conf // 2)
    y_v = _conv_v(h_conf, conf_dw_w[0], conf // 2, conf + conf // 2, conf // 2, seq)
    cat = _mix(o_f, o_b, h_qkvz, z_off // dn, y_h, y_v, dn_norm_g[0], conf_dw_b[0], conf_ln_g[0],
               conf_ln_b[0], tl)
    y = _matmul(cat, w_o, BF16)
    u1, x1 = _mlp_in(xr, y, ln_in_g, ln_in_b, ln1_g[0], ln1_b[0], mod3, lat_row, alpha, tl)
    hid = _matmul(u1, w1, BF16, bias=b_mlp1[0])
    y2 = _matmul_k(hid, w2, BF16)
    out = _final(x1, y2, b_mlp2[0], ln2_g[0], ln2_b[0], mod3, lat_row, alpha, tl)
    return out.reshape(bsz, seq, d)
```

```python
import functools

import jax
import jax.numpy as jnp
from jax import lax
from jax.experimental import pallas as pl
from jax.experimental.pallas import tpu as pltpu

F32 = jnp.float32
BF16 = jnp.bfloat16

GRID_W = 64
CHUNK = 64
INV_BASE = 8
SHORT_CONV = 7
CONF_K = 31
LANES = 128
SUBLANES = 8
BF16_ROWS = 16
VMEM_LIMIT = 56 * 1024 * 1024
LN_EPS = 1e-5


def _params(sem):
    return pltpu.CompilerParams(dimension_semantics=sem, vmem_limit_bytes=VMEM_LIMIT)


def _tile(n, preferred):
    t = min(preferred, n)
    while n % t:
        t //= 2
    return t


def _silu(x):
    return x * jax.nn.sigmoid(x)


def _ln(x, g, b):
    mu = jnp.mean(x, axis=-1, keepdims=True)
    xc = x - mu
    var = jnp.mean(xc * xc, axis=-1, keepdims=True)
    return xc * lax.rsqrt(var + LN_EPS) * g + b


def _mod_kernel(c_ref, w_ref, b_ref, o_ref):
    s = _silu(c_ref[...]).astype(BF16)
    o_ref[...] = jnp.dot(s, w_ref[...].astype(BF16), preferred_element_type=F32) + b_ref[...]


def _mod_table(cc, w, b, tn=512):
    r, d = cc.shape
    n = w.shape[1]
    return pl.pallas_call(
        _mod_kernel,
        grid=(n // tn,),
        in_specs=[pl.BlockSpec((r, d), lambda j: (0, 0)),
                  pl.BlockSpec((d, tn), lambda j: (0, j)),
                  pl.BlockSpec((1, tn), lambda j: (0, j))],
        out_specs=pl.BlockSpec((r, tn), lambda j: (0, j)),
        out_shape=jax.ShapeDtypeStruct((r, n), F32),
        compiler_params=_params(("parallel",)),
        name="mod_table",
    )(cc, w, b.reshape(1, n))


def _mod_spec(d, row_of_tile, chunk):
    return pl.BlockSpec((None, 1, d), lambda i: (row_of_tile(i), 0, chunk))


def _ln_mod_kernel(x_ref, g_ref, b_ref, sh_ref, sc_ref, o_ref):
    xn = _ln(x_ref[...], g_ref[...], b_ref[...])
    o_ref[...] = (xn * (1.0 + sc_ref[...]) + sh_ref[...]).astype(o_ref.dtype)


def _ln_mod(x, g, b, mod3, row_of_tile, tl=256):
    m, d = x.shape
    row = pl.BlockSpec((tl, d), lambda i: (i, 0))
    vec = pl.BlockSpec((1, d), lambda i: (0, 0))
    return pl.pallas_call(
        _ln_mod_kernel,
        grid=(m // tl,),
        in_specs=[row, vec, vec, _mod_spec(d, row_of_tile, 0), _mod_spec(d, row_of_tile, 1)],
        out_specs=row,
        out_shape=jax.ShapeDtypeStruct((m, d), BF16),
        compiler_params=_params(("parallel",)),
        name="ln_mod",
    )(x, g.reshape(1, d), b.reshape(1, d), mod3, mod3)


def _mlp_in_kernel(x_ref, y_ref, gin_ref, bin_ref, ga_ref, g1_ref, b1_ref, sh_ref, sc_ref, u_ref, x1_ref,
                   *, alpha):
    xn = _ln(x_ref[...], gin_ref[...], bin_ref[...])
    x1 = _ln(alpha * xn + ga_ref[...] * y_ref[...].astype(F32), g1_ref[...], b1_ref[...])
    x1_ref[...] = x1
    u_ref[...] = (x1 * (1.0 + sc_ref[...]) + sh_ref[...]).astype(u_ref.dtype)


def _mlp_in(x, y, gin, bin_, g1, b1, mod3, row_of_tile, alpha, tl=256):
    m, d = x.shape
    row = pl.BlockSpec((tl, d), lambda i: (i, 0))
    vec = pl.BlockSpec((1, d), lambda i: (0, 0))
    v = lambda a: a.reshape(1, d)
    return pl.pallas_call(
        functools.partial(_mlp_in_kernel, alpha=alpha),
        grid=(m // tl,),
        in_specs=[row, row, vec, vec, _mod_spec(d, row_of_tile, 2), vec, vec,
                  _mod_spec(d, row_of_tile, 3), _mod_spec(d, row_of_tile, 4)],
        out_specs=[row, row],
        out_shape=[jax.ShapeDtypeStruct((m, d), BF16), jax.ShapeDtypeStruct((m, d), F32)],
        compiler_params=_params(("parallel",)),
        name="mlp_in",
    )(x, y, v(gin), v(bin_), mod3, v(g1), v(b1), mod3, mod3)


def _final_kernel(x1_ref, y2_ref, bm_ref, gm_ref, g2_ref, b2_ref, o_ref, *, alpha):
    y2 = y2_ref[...].astype(F32) + bm_ref[...]
    o_ref[...] = _ln(alpha * x1_ref[...] + gm_ref[...] * y2, g2_ref[...], b2_ref[...])


def _final(x1, y2, bm, g2, b2, mod3, row_of_tile, alpha, tl=256):
    m, d = x1.shape
    row = pl.BlockSpec((tl, d), lambda i: (i, 0))
    vec = pl.BlockSpec((1, d), lambda i: (0, 0))
    v = lambda a: a.reshape(1, d)
    return pl.pallas_call(
        functools.partial(_final_kernel, alpha=alpha),
        grid=(m // tl,),
        in_specs=[row, row, vec, _mod_spec(d, row_of_tile, 5), vec, vec],
        out_specs=row,
        out_shape=jax.ShapeDtypeStruct((m, d), F32),
        compiler_params=_params(("parallel",)),
        name="final_ln",
    )(x1, y2, v(bm), mod3, v(g2), v(b2))


def _mm_kernel(a_ref, w_ref, *rest, relu2, n_cast):
    rest = list(rest)
    b_ref = rest.pop(0) if relu2 else None
    cast_src, o_ref, cast_dst = rest[:n_cast], rest[n_cast], rest[n_cast + 1:]
    h = jnp.dot(a_ref[...], w_ref[...], preferred_element_type=F32)
    if relu2:
        r = jnp.maximum(h + b_ref[...], 0.0)
        h = r * r
    o_ref[...] = h.astype(o_ref.dtype)
    for src, dst in zip(cast_src, cast_dst):
        dst[...] = src[...].astype(dst.dtype)


def _matmul(a, w, out_dtype, n_cols=None, bias=None, cast=(), tm=1024, tn=1024):
    m, k = a.shape
    n = w.shape[1] if n_cols is None else n_cols
    tm, tn = _tile(m, tm), _tile(n, tn)
    gi, gj = m // tm, n // tn
    specs = [pl.BlockSpec((tm, k), lambda i, j: (i, 0)), pl.BlockSpec((k, tn), lambda i, j: (0, j))]
    args = (a, w)
    if bias is not None:
        specs.append(pl.BlockSpec((1, tn), lambda i, j: (0, j)))
        args += (bias.reshape(1, -1),)
    out_specs = [pl.BlockSpec((tm, tn), lambda i, j: (i, j))]
    out_shape = [jax.ShapeDtypeStruct((m, n), out_dtype)]
    for src in cast:
        rows, cols = src.shape
        assert rows % (gi * gj * BF16_ROWS) == 0
        slab = pl.BlockSpec((rows // (gi * gj), cols), lambda i, j: (i * gj + j, 0))
        specs.append(slab)
        args += (src,)
        out_specs.append(slab)
        out_shape.append(jax.ShapeDtypeStruct((rows, cols), BF16))
    res = pl.pallas_call(
        functools.partial(_mm_kernel, relu2=bias is not None, n_cast=len(cast)),
        grid=(gi, gj),
        in_specs=specs,
        out_specs=out_specs,
        out_shape=out_shape,
        compiler_params=_params(("parallel", "arbitrary")),
        name="matmul",
    )(*args)
    return res if cast else res[0]


def _mmk_kernel(a_ref, w_ref, o_ref, acc_ref):
    kk = pl.program_id(2)

    @pl.when(kk == 0)
    def _():
        acc_ref[...] = jnp.zeros_like(acc_ref)

    acc_ref[...] += jnp.dot(a_ref[...], w_ref[...], preferred_element_type=F32)

    @pl.when(kk == pl.num_programs(2) - 1)
    def _():
        o_ref[...] = acc_ref[...].astype(o_ref.dtype)


def _matmul_k(a, w, out_dtype, tm=1024, tn=2048, tk=2048):
    m, k = a.shape
    n = w.shape[1]
    tm, tn, tk = _tile(m, tm), _tile(n, tn), _tile(k, tk)
    return pl.pallas_call(
        _mmk_kernel,
        grid=(m // tm, n // tn, k // tk),
        in_specs=[pl.BlockSpec((tm, tk), lambda i, j, kk: (i, kk)),
                  pl.BlockSpec((tk, tn), lambda i, j, kk: (kk, j))],
        out_specs=pl.BlockSpec((tm, tn), lambda i, j, kk: (i, j)),
        out_shape=jax.ShapeDtypeStruct((m, n), out_dtype),
        scratch_shapes=[pltpu.VMEM((tm, tn), F32)],
        compiler_params=_params(("parallel", "parallel", "arbitrary")),
        name="matmul_k",
    )(a, w)


def _shift_matrix(blk, win):
    half = SHORT_CONV // 2
    cols = -(-SHORT_CONV * win // LANES) * LANES
    r = jnp.arange(blk)[:, None]
    c = jnp.arange(cols)[None, :]
    hit = c < 0
    for tap in range(SHORT_CONV):
        hit = hit | (c == tap * win + r + BF16_ROWS - half + tap)
    return hit.astype(BF16)


def _prep_kernel(prev_ref, main_ref, next_ref, w_ref, shift_ref, o_ref, *, tiles_per_seq, norm_blocks):
    i, j = pl.program_id(0), pl.program_id(1)
    t = i % tiles_per_seq
    tl, tc = main_ref.shape
    blk = min(LANES, tl)
    win = blk + 2 * BF16_ROWS
    prev = jnp.where(t > 0, prev_ref[...], 0.0)
    nxt = jnp.where(t < tiles_per_seq - 1, next_ref[...], 0.0)
    ext = jnp.concatenate([prev, main_ref[...], nxt], axis=0)
    w16 = w_ref[...].astype(BF16)
    scaled = [ext * w16[tap:tap + 1, :] for tap in range(SHORT_CONV)]
    shift = shift_ref[...]
    pad = jnp.zeros((shift.shape[1] - SHORT_CONV * win, tc), BF16)
    for b in range(tl // blk):
        rows = slice(b * blk, (b + 1) * blk)
        stacked = jnp.concatenate([z[b * blk:b * blk + win, :] for z in scaled] + [pad], axis=0)
        y = _silu(jnp.dot(shift, stacked, preferred_element_type=F32))
        for h in range(tc // LANES):
            lanes = slice(h * LANES, (h + 1) * LANES)
            yh = y[:, lanes]
            ss = jnp.sum(yh * yh, axis=-1, keepdims=True)
            yn = yh * lax.rsqrt(ss + 1e-6)
            o_ref[rows, lanes] = jnp.where(j < norm_blocks, yn, yh).astype(o_ref.dtype)


def _prep(h_main, w_conv, seq_len, dn, tl=512, tc=1024):
    m = h_main.shape[0]
    tl = min(tl, seq_len)
    tiles_per_seq = seq_len // tl
    hb = tl // BF16_ROWS
    last = m // BF16_ROWS - 1
    blk = min(LANES, tl)
    shift = _shift_matrix(blk, blk + 2 * BF16_ROWS)
    return pl.pallas_call(
        functools.partial(_prep_kernel, tiles_per_seq=tiles_per_seq, norm_blocks=2 * dn // tc),
        grid=(m // tl, 3 * dn // tc),
        in_specs=[pl.BlockSpec((BF16_ROWS, tc), lambda i, j: (jnp.maximum(i * hb - 1, 0), j)),
                  pl.BlockSpec((tl, tc), lambda i, j: (i, j)),
                  pl.BlockSpec((BF16_ROWS, tc), lambda i, j: (jnp.minimum((i + 1) * hb, last), j)),
                  pl.BlockSpec((SHORT_CONV, tc), lambda i, j: (0, j)),
                  pl.BlockSpec(shift.shape, lambda i, j: (0, 0))],
        out_specs=pl.BlockSpec((tl, tc), lambda i, j: (i, j)),
        out_shape=jax.ShapeDtypeStruct((m, 3 * dn), BF16),
        compiler_params=_params(("parallel", "parallel")),
        name="prep_qkv",
    )(h_main, h_main, h_main, w_conv, shift)


def _cumsum_rows(x, reverse):
    n = x.shape[0]
    row = lax.broadcasted_iota(jnp.int32, x.shape, 0)
    s = 1
    while s < n:
        if reverse:
            x = x + jnp.where(row < n - s, pltpu.roll(x, n - s, axis=0), 0.0)
        else:
            x = x + jnp.where(row >= s, pltpu.roll(x, s, axis=0), 0.0)
        s *= 2
    return x


def _dot(a, b):
    return jnp.dot(a.astype(BF16), b.astype(BF16), preferred_element_type=F32)


def _delta_units(q_ref, k_ref, v_ref, g_ref, par_ref, s_ref, o_ref, *, rows, hb, reverse, scale):
    c_len = rows.stop - rows.start
    graw = g_ref[rows, :]
    z = graw + par_ref[1:2, :]
    softplus = jnp.maximum(z, 0.0) + jnp.log1p(jnp.exp(-jnp.abs(z)))
    gc = _cumsum_rows(-jnp.exp(par_ref[0:1, :]) * softplus, reverse)
    beta = jax.nn.sigmoid(graw)
    g_last = gc[0:1, :] if reverse else gc[c_len - 1:c_len, :]
    e_in = jnp.exp(gc)
    e_out = jnp.exp(g_last - gc)
    d_last = jnp.exp(g_last)
    gc_t = jnp.transpose(jnp.concatenate([gc, gc], axis=0))
    row = lax.broadcasted_iota(jnp.int32, (c_len, LANES), 0)
    lane = lax.broadcasted_iota(jnp.int32, (c_len, LANES), 1)
    second = lane >= c_len
    col = jnp.where(second, lane - c_len, lane)
    incl = (row <= col) if reverse else (row >= col)
    strict = (row < col) if reverse else (row > col)
    g_col0 = 2 * hb if reverse else 0
    zeros = jnp.zeros((c_len, LANES), BF16)

    def per_head(t, c0, c1, rows):
        return jnp.concatenate([jnp.broadcast_to(t[:, c0:c0 + 1], (rows, LANES)),
                                jnp.broadcast_to(t[:, c1:c1 + 1], (rows, LANES))], axis=1)

    pairs = []
    for p in range(hb // 2):
        cg0, cg1 = g_col0 + 2 * p, g_col0 + 2 * p + 1
        lanes = slice(2 * p * LANES, (2 * p + 2) * LANES)
        k16 = k_ref[rows, lanes]
        q = q_ref[rows, lanes].astype(F32) * scale
        k = k16.astype(F32)
        b = per_head(beta, cg0 + hb, cg1 + hb, c_len)
        ein = per_head(e_in, cg0, cg1, c_len)
        diff = (jnp.where(second, gc[:, cg1:cg1 + 1], gc[:, cg0:cg0 + 1])
                - jnp.where(second[0:1], gc_t[cg1:cg1 + 1, :], gc_t[cg0:cg0 + 1, :]))
        kbeta = k * b
        vb = (v_ref[rows, lanes].astype(F32) * b).astype(BF16)
        kbe = (kbeta * ein).astype(BF16)
        pairs.append(dict(
            pi=p, rows=rows, lanes=lanes, s_ref=s_ref, o_ref=o_ref, strict=strict,
            eye=jnp.where(row == col, 1.0, 0.0),
            gamma=jnp.where(incl, jnp.exp(jnp.where(incl, diff, 0.0)), 0.0),
            lhs1=jnp.concatenate([kbeta, q], axis=0).astype(BF16),
            k_bd=jnp.concatenate([jnp.concatenate([k16[:, :LANES], zeros], axis=1),
                                  jnp.concatenate([zeros, k16[:, LANES:]], axis=1)], axis=0),
            rhs_bd=jnp.concatenate(
                [jnp.concatenate([vb[:, :LANES], zeros, kbe[:, :LANES], zeros], axis=1),
                 jnp.concatenate([zeros, vb[:, LANES:], zeros, kbe[:, LANES:]], axis=1)], axis=0),
            qd=(q * ein).astype(BF16),
            kd=k * per_head(e_out, cg0, cg1, c_len),
            dl=per_head(d_last, cg0, cg1, 1)))
    return pairs


def _lane_group_masks(c_len):
    lane = lax.broadcasted_iota(jnp.int32, (c_len, LANES), 1)
    return [(lane >= g * c_len) & (lane < (g + 1) * c_len) for g in range(LANES // c_len)]


def _level_masks(c_len, width):
    r = lax.broadcasted_iota(jnp.int32, (c_len, width), 0)
    c = jnp.bitwise_and(lax.broadcasted_iota(jnp.int32, (c_len, width), 1), c_len - 1)

    def same(s):
        sh = s.bit_length() - 1
        return lax.shift_right_logical(r, sh) == lax.shift_right_logical(c, sh)

    masks = [same(INV_BASE)]
    s = INV_BASE
    while s < c_len:
        masks.append(same(2 * s) & jnp.logical_not(same(s)))
        s *= 2
    return masks


def _block_diag(x16, group_masks):
    c_len, width = x16.shape
    per_tile = LANES // c_len
    zeros = jnp.zeros((c_len, LANES), x16.dtype)
    blocks = []
    for i in range(width // c_len):
        t, g = divmod(i, per_tile)
        tile = jnp.where(group_masks[g], x16[:, t * LANES:(t + 1) * LANES], 0.0)
        blocks.append(jnp.concatenate([tile if tt == t else zeros for tt in range(width // LANES)], axis=1))
    return jnp.concatenate(blocks, axis=0)


def _delta_kernel(*refs, hb, scale, write_o):
    (qf, kf, vf, gf, qb, kb, vb, gb, par, s0f, s0b), outs = refs[:11], refs[11:]
    if write_o:
        of, ob, sf, sb = outs
    else:
        (sf, sb), of, ob = outs, None, None

    @pl.when(pl.program_id(2) == 0)
    def _():
        sf[...] = s0f[...]
        sb[...] = s0b[...]

    c_len = CHUNK
    n_sub = qf.shape[0] // c_len
    phases = []
    for j in range(n_sub):
        rf = slice(j * c_len, (j + 1) * c_len)
        rb = slice((n_sub - 1 - j) * c_len, (n_sub - j) * c_len)
        phases.append(_delta_units(qf, kf, vf, gf, par, sf, of, rows=rf, hb=hb, reverse=False, scale=scale)
                      + _delta_units(qb, kb, vb, gb, par, sb, ob, rows=rb, hb=hb, reverse=True, scale=scale))
    units = [u for ph in phases for u in ph]
    dn_t = (((1,), (1,)), ((), ()))

    for u in units:
        m1 = lax.dot_general(u["lhs1"], u["k_bd"], dn_t, preferred_element_type=F32)
        u["a"] = jnp.where(u["strict"], m1[:c_len] * u["gamma"], 0.0)
        u["attn"] = (m1[c_len:] * u["gamma"]).astype(BF16)
    quads = [dict(pairs=units[i:i + 2]) for i in range(0, len(units), 2)]
    group_masks = _lane_group_masks(c_len)
    lvl_masks = {}
    for qd in quads:
        a = jnp.concatenate([u["a"] for u in qd["pairs"]], axis=1)
        width = a.shape[1]
        if width not in lvl_masks:
            lvl_masks[width] = _level_masks(c_len, width)
        qd["same"] = group_masks
        same_base, qd["levels"] = lvl_masks[width][0], lvl_masks[width][1:]
        qd["a"] = a
        d = jnp.where(same_base, a, 0.0)
        qd["t"] = jnp.concatenate([u["eye"] for u in qd["pairs"]], axis=1) - d
        d16 = d.astype(BF16)
        qd["dpow"] = jnp.dot(d16, _block_diag(d16, qd["same"]), preferred_element_type=F32).astype(BF16)
    rounds = INV_BASE.bit_length() - 2
    for i in range(rounds):
        for qd in quads:
            bd = _block_diag(qd["dpow"], qd["same"])
            if i + 1 < rounds:
                res = jnp.dot(jnp.concatenate([qd["t"].astype(BF16), qd["dpow"]], axis=0), bd,
                              preferred_element_type=F32)
                qd["t"] = qd["t"] + res[:c_len]
                qd["dpow"] = res[c_len:].astype(BF16)
            else:
                qd["t"] = qd["t"] + jnp.dot(qd["t"].astype(BF16), bd, preferred_element_type=F32)
    for lvl in range(len(quads[0]["levels"])):
        for qd in quads:
            a_s = jnp.where(qd["levels"][lvl], qd["a"], 0.0).astype(BF16)
            qd["n"] = jnp.dot(a_s, _block_diag(qd["t"].astype(BF16), qd["same"]),
                              preferred_element_type=F32).astype(BF16)
        for qd in quads:
            qd["t"] = qd["t"] - jnp.dot(qd["t"].astype(BF16), _block_diag(qd["n"], qd["same"]),
                                        preferred_element_type=F32)
    for qd in quads:
        for j, u in enumerate(qd["pairs"]):
            t16 = qd["t"][:, 2 * j * c_len:2 * (j + 1) * c_len].astype(BF16)
            u["uw"] = jnp.dot(t16, u["rhs_bd"], preferred_element_type=F32)
    zeros = jnp.zeros((LANES, LANES), BF16)
    zc = jnp.zeros((c_len, LANES), BF16)
    for u in units:
        u["kd_t"] = jnp.transpose(
            jnp.concatenate([u["kd"][:, :LANES], u["kd"][:, LANES:]], axis=0)).astype(BF16)
    for phase in phases:
        for u in phase:
            s = u["s_ref"][u["pi"]]
            s16 = s.astype(BF16)
            s_bd = jnp.concatenate([jnp.concatenate([s16[:, :LANES], zeros], axis=1),
                                    jnp.concatenate([zeros, s16[:, LANES:]], axis=1)], axis=0)
            w = u["uw"][:, 2 * LANES:].astype(BF16)
            ws = jnp.dot(jnp.concatenate([w, u["qd"]], axis=0), s_bd, preferred_element_type=F32)
            vn = (u["uw"][:, :2 * LANES] - ws[:c_len]).astype(BF16)
            u["vn_bd"] = jnp.concatenate([jnp.concatenate([vn[:, :LANES], zc], axis=1),
                                          jnp.concatenate([zc, vn[:, LANES:]], axis=1)], axis=0)
            u["o"] = ws[c_len:]
            u["s"] = s
        for u in phase:
            if u["o_ref"] is None:
                ds = jnp.dot(u["kd_t"], u["vn_bd"], preferred_element_type=F32)
            else:
                res = jnp.dot(jnp.concatenate([u["attn"], u["kd_t"]], axis=0), u["vn_bd"],
                              preferred_element_type=F32)
                u["o_ref"][u["rows"], u["lanes"]] = (u["o"] + res[:c_len]).astype(u["o_ref"].dtype)
                ds = res[c_len:]
            u["s_ref"][u["pi"]] = u["s"] * u["dl"] + ds


def _delta(qkv, gates, par, s0f, s0b, seq_len, nh, write_o, hb=4, rows=4 * CHUNK):
    m = qkv.shape[0]
    bsz = m // seq_len
    rows = _tile(seq_len, rows)
    assert hb % 4 == 0 and rows % CHUNK == 0
    nc = seq_len // rows
    nhb = nh // hb
    w = hb * LANES
    fwd = lambda b, g, c: b * nc + c
    bwd = lambda b, g, c: b * nc + nc - 1 - c

    def qkv_specs(row):
        return [pl.BlockSpec((rows, w), lambda b, g, c, o=o: (row(b, g, c), o * nhb + g)) for o in range(3)]

    g_spec = lambda row: pl.BlockSpec((rows, LANES), lambda b, g, c: (row(b, g, c), g))
    s_spec = pl.BlockSpec((None, hb // 2, LANES, 2 * LANES), lambda b, g, c: (b, g, 0, 0))
    o_spec = lambda row: pl.BlockSpec((rows, w), lambda b, g, c: (row(b, g, c), g))
    s_shape = jax.ShapeDtypeStruct((bsz, nh // 2, LANES, 2 * LANES), F32)
    o_shape = jax.ShapeDtypeStruct((m, nh * LANES), BF16)
    out_specs = [s_spec, s_spec]
    out_shape = [s_shape, s_shape]
    if write_o:
        out_specs = [o_spec(fwd), o_spec(bwd)] + out_specs
        out_shape = [o_shape, o_shape] + out_shape
    return pl.pallas_call(
        functools.partial(_delta_kernel, hb=hb, scale=float(LANES) ** -0.5, write_o=write_o),
        grid=(bsz, nhb, nc),
        in_specs=qkv_specs(fwd) + [g_spec(fwd)] + qkv_specs(bwd) + [g_spec(bwd)]
        + [pl.BlockSpec((None, 8, LANES), lambda b, g, c: (g, 0, 0)), s_spec, s_spec],
        out_specs=out_specs,
        out_shape=out_shape,
        compiler_params=_params(("parallel", "parallel", "arbitrary")),
        name="delta_rule",
    )(qkv, qkv, qkv, gates, qkv, qkv, qkv, gates, par, s0f, s0b)


def _glu(v_ref, g_ref, rows):
    return v_ref[rows, :].astype(F32) * jax.nn.sigmoid(g_ref[rows, :].astype(F32))


def _conv_h_kernel(v_ref, g_ref, w_ref, o_ref, pad_ref):
    rows, tc = v_ref.shape
    half = CONF_K // 2
    lead = 16
    stride = lead + GRID_W + 16
    total = rows // GRID_W * stride
    w = w_ref[...]
    for g in range(rows // GRID_W):
        base = g * stride
        pad_ref[0, base:base + lead, :] = jnp.zeros((lead, tc), F32)
        pad_ref[0, base + lead + GRID_W:base + stride, :] = jnp.zeros((16, tc), F32)
        pad_ref[0, base + lead:base + lead + GRID_W, :] = _glu(v_ref, g_ref,
                                                               slice(g * GRID_W, (g + 1) * GRID_W))
    for s in range(1, SUBLANES):
        pad_ref[s, 0:total - SUBLANES, :] = pad_ref[0, s:total - SUBLANES + s, :]
    for g in range(rows // GRID_W):
        for cs in range(tc // LANES):
            lanes = slice(cs * LANES, (cs + 1) * LANES)
            acc = None
            for tap in range(CONF_K):
                off = lead - half + tap
                s = off % SUBLANES
                start = g * stride + off - s
                term = w[tap:tap + 1, lanes] * pad_ref[s, start:start + GRID_W, lanes]
                acc = term if acc is None else acc + term
            o_ref[g * GRID_W:(g + 1) * GRID_W, lanes] = acc.astype(o_ref.dtype)


def _conv_h(h_main, w_dw, col_v, col_g, half_c, rows=512, tc=256):
    m = h_main.shape[0]
    ov, og = col_v // tc, col_g // tc
    return pl.pallas_call(
        _conv_h_kernel,
        grid=(m // rows, half_c // tc),
        in_specs=[pl.BlockSpec((rows, tc), lambda i, j: (i, ov + j)),
                  pl.BlockSpec((rows, tc), lambda i, j: (i, og + j)),
                  pl.BlockSpec((CONF_K, tc), lambda i, j: (0, j))],
        out_specs=pl.BlockSpec((rows, tc), lambda i, j: (i, j)),
        out_shape=jax.ShapeDtypeStruct((m, half_c), BF16),
        scratch_shapes=[pltpu.VMEM((SUBLANES, rows // GRID_W * (GRID_W + 32), tc), F32)],
        compiler_params=_params(("parallel", "parallel")),
        name="conv_rows",
    )(h_main, h_main, w_dw)


def _conv_v_kernel(v_ref, g_ref, w_ref, o_ref, pad_ref):
    seq, tc = v_ref.shape
    halo = (CONF_K // 2) * GRID_W
    pad_ref[0:halo, :] = jnp.zeros((halo, tc), F32)
    pad_ref[halo + seq:halo + seq + halo, :] = jnp.zeros((halo, tc), F32)
    blk = 512

    def fill(i, carry):
        r0 = pl.multiple_of(i * blk, blk)
        pad_ref[pl.ds(halo + r0, blk), :] = _glu(v_ref, g_ref, pl.ds(r0, blk))
        return carry

    lax.fori_loop(0, seq // blk, fill, 0)
    w = w_ref[...]

    def body(i, carry):
        r0 = pl.multiple_of(i * GRID_W, GRID_W)
        for cs in range(tc // LANES):
            lanes = slice(cs * LANES, (cs + 1) * LANES)
            acc = None
            for tap in range(CONF_K):
                term = w[tap:tap + 1, lanes] * pad_ref[pl.ds(r0 + tap * GRID_W, GRID_W), lanes]
                acc = term if acc is None else acc + term
            o_ref[pl.ds(r0, GRID_W), lanes] = acc.astype(o_ref.dtype)
        return carry

    lax.fori_loop(0, seq // GRID_W, body, 0)


def _conv_v(h_main, w_dw, col_v, col_g, half_c, seq_len, tc=256):
    m = h_main.shape[0]
    ov, og = col_v // tc, col_g // tc
    ow = half_c // tc
    return pl.pallas_call(
        _conv_v_kernel,
        grid=(m // seq_len, half_c // tc),
        in_specs=[pl.BlockSpec((seq_len, tc), lambda b, j: (b, ov + j)),
                  pl.BlockSpec((seq_len, tc), lambda b, j: (b, og + j)),
                  pl.BlockSpec((CONF_K, tc), lambda b, j: (0, ow + j))],
        out_specs=pl.BlockSpec((seq_len, tc), lambda b, j: (b, j)),
        out_shape=jax.ShapeDtypeStruct((m, half_c), BF16),
        scratch_shapes=[pltpu.VMEM((seq_len + 2 * (CONF_K // 2) * GRID_W, tc), F32)],
        compiler_params=_params(("parallel", "parallel")),
        name="conv_cols",
    )(h_main, h_main, w_dw)


def _mix_kernel(of_ref, ob_ref, z_ref, yh_ref, yv_ref, gn_ref, bdw_ref, lg_ref, lb_ref, o_ref):
    dn = of_ref.shape[1]
    for h in range(dn // LANES):
        lanes = slice(h * LANES, (h + 1) * LANES)
        o = of_ref[:, lanes].astype(F32) + ob_ref[:, lanes].astype(F32)
        ms = jnp.mean(o * o, axis=-1, keepdims=True)
        on = o * lax.rsqrt(ms + 1e-6) * gn_ref[...]
        o_ref[:, lanes] = (on * _silu(z_ref[:, lanes].astype(F32))).astype(o_ref.dtype)
    y = jnp.concatenate([yh_ref[...].astype(F32), yv_ref[...].astype(F32)], axis=1) + bdw_ref[...]
    o_ref[:, dn:] = _silu(_ln(y, lg_ref[...], lb_ref[...])).astype(o_ref.dtype)


def _mix(o_f, o_b, h_main, z_blk, y_h, y_v, gn, bdw, lg, lb, tl=256):
    m, dn = o_f.shape
    half_c = y_h.shape[1]
    conf = 2 * half_c
    row = lambda w: pl.BlockSpec((tl, w), lambda i: (i, 0))
    vec = lambda w: pl.BlockSpec((1, w), lambda i: (0, 0))
    return pl.pallas_call(
        _mix_kernel,
        grid=(m // tl,),
        in_specs=[row(dn), row(dn), pl.BlockSpec((tl, dn), lambda i: (i, z_blk)), row(half_c), row(half_c),
                  vec(LANES), vec(conf), vec(conf), vec(conf)],
        out_specs=row(dn + conf),
        out_shape=jax.ShapeDtypeStruct((m, dn + conf), BF16),
        compiler_params=_params(("parallel",)),
        name="mixer_in",
    )(o_f, o_b, h_main, y_h, y_v, gn.reshape(1, LANES), bdw.reshape(1, conf), lg.reshape(1, conf),
      lb.reshape(1, conf))


def kernel(x, c, ctx, c_ctx, ln_in_g, ln_in_b, w_mod, b_mod, w_in, w_qkv_conv, a_log_f, dt_bias_f,
           a_log_b, dt_bias_b, dn_norm_g, conf_dw_w, conf_dw_b, conf_ln_g, conf_ln_b, w_out, ln1_g, ln1_b,
           w_mlp1, b_mlp1, w_mlp2, b_mlp2, ln2_g, ln2_b):
    assert w_mod.shape[0] == 1, "single-layer trunk only"
    bsz, seq, d = x.shape
    ctx_len = ctx.shape[1]
    nh = a_log_f.shape[1]
    dn = d // 2
    conf = d - dn
    assert dn == nh * LANES and seq % GRID_W == 0 and ctx_len % CHUNK == 0
    z_off, g_off, conf_off = 3 * dn, 4 * dn, 4 * dn + 4 * nh
    alpha = 2.0 ** 0.25
    hb = _tile(nh, 16)
    nhb = nh // hb
    m, mc = bsz * seq, bsz * ctx_len

    wi = w_in[0]
    w_in16 = wi.astype(BF16)
    w_conf = w_in16[:, conf_off:]
    wg = wi[:, g_off:conf_off].reshape(d, 4, nhb, hb).transpose(0, 2, 1, 3).reshape(d, nhb, 4 * hb)
    wg = jnp.pad(wg, ((0, 0), (0, 0), (0, LANES - 4 * hb))).reshape(d, nhb * LANES).astype(BF16)
    zeros = jnp.zeros_like(a_log_f[0])
    par = jnp.stack([jnp.stack([a_log_f[0], zeros, a_log_b[0], zeros]),
                     jnp.stack([dt_bias_f[0], zeros, dt_bias_b[0], zeros])])
    par = par.reshape(2, 4, nhb, hb).transpose(2, 0, 1, 3).reshape(nhb, 2, 4 * hb)
    par = jnp.pad(par, ((0, 0), (0, 6), (0, LANES - 4 * hb)))

    cc = jnp.concatenate([c, c_ctx[None, :], jnp.zeros((8 - bsz - 1, d), F32)], axis=0)
    mod3 = _mod_table(cc, w_mod[0], b_mod[0]).reshape(8, 1, 6 * d)

    tl = 256
    lat_row = lambda i: (i * tl) // seq
    ctx_row = lambda i: bsz

    uc = _ln_mod(ctx.reshape(mc, d), ln_in_g, ln_in_b, mod3, ctx_row, tl)
    hc = _matmul(uc, w_in16, BF16, n_cols=3 * dn)
    gates_c = _matmul(uc, wg, F32, tn=nhb * LANES)
    qkv_c = _prep(hc, w_qkv_conv[0], ctx_len, dn)
    s0 = jnp.zeros((bsz, nh // 2, LANES, 2 * LANES), F32)
    s_f, s_b = _delta(qkv_c, gates_c, par, s0, s0, ctx_len, nh, False, hb)

    xr = x.reshape(m, d)
    u0 = _ln_mod(xr, ln_in_g, ln_in_b, mod3, lat_row, tl)
    h_qkvz, w1, w_o = _matmul(u0, w_in16, BF16, n_cols=g_off, cast=(w_mlp1[0], w_out[0]))
    h_conf, w2 = _matmul(u0, w_conf, BF16, cast=(w_mlp2[0],))
    gates = _matmul(u0, wg, F32, tn=nhb * LANES)
    qkv = _prep(h_qkvz, w_qkv_conv[0], seq, dn)
    o_f, o_b, _, _ = _delta(qkv, gates, par, s_f, s_b, seq, nh, True, hb)
    y_h = _conv_h(h_conf, conf_dw_w[0], 0, conf, conf // 2)
    y_v = _conv_v(h_conf, conf_dw_w[0], conf // 2, conf + conf // 2, conf // 2, seq)
    cat = _mix(o_f, o_b, h_qkvz, z_off // dn, y_h, y_v, dn_norm_g[0], conf_dw_b[0], conf_ln_g[0],
               conf_ln_b[0], tl)
    y = _matmul(cat, w_o, BF16)
    u1, x1 = _mlp_in(xr, y, ln_in_g, ln_in_b, ln1_g[0], ln1_b[0], mod3, lat_row, alpha, tl)
    hid = _matmul(u1, w1, BF16, bias=b_mlp1[0])
    y2 = _matmul_k(hid, w2, BF16)
    out = _final(x1, y2, b_mlp2[0], ln2_g[0], ln2_b[0], mod3, lat_row, alpha, tl)
    return out.reshape(bsz, seq, d)
```
